```python
import math
import jax, jax.numpy as jnp
from jax import lax
import numpy as np


D_MODEL = 2048
BATCH = 2
SEQ = 8192
DEPTH = 2

HEAD_DIM = 128
N_MEM = 256
MEM_HEADS = 4
MEM_WIDTH = MEM_HEADS * HEAD_DIM
MIX_WIDTH = D_MODEL - MEM_WIDTH
MOBA_HEADS = MIX_WIDTH // HEAD_DIM
MOBA_BLOCK = 256
MOBA_TOPK = 3
MOBA_QCHUNK = 32
DIFF_HEADS = MIX_WIDTH // (2 * HEAD_DIM)
DIFF_VDIM = 2 * HEAD_DIM
DENSE_QBLOCK = 128
REL_BUCKETS = 32
REL_MAX_DIST = 128
N_BIAS_MAPS = MOBA_HEADS
N_GROUPS = 8
EXPERTS_PER_GROUP = 8
N_EXPERTS = N_GROUPS * EXPERTS_PER_GROUP
TOPK_IN_GROUP = 2
EXPERT_FF = D_MODEL // 4
MOE_BLOCK = 256
N_A = DEPTH // 2
N_B = DEPTH - N_A
RMS_EPS = 1e-6

kernel_name = 'yoco_moba_diffattn_hmoe'


def rmsnorm(x, g):
    xf = x.astype(jnp.float32)
    y = xf * lax.rsqrt(jnp.mean(xf * xf, axis=-1, keepdims=True) + RMS_EPS)
    return (y * g.astype(jnp.float32)).astype(x.dtype)


def split_heads(t, n_heads):
    b, s, _ = t.shape
    return t.reshape(b, s, n_heads, -1).transpose(0, 2, 1, 3)


def merge_heads(t):
    b, h, s, d = t.shape
    return t.transpose(0, 2, 1, 3).reshape(b, s, h * d)


def t5_bucket(dist):
    n = jnp.maximum(dist, 0)
    max_exact = REL_BUCKETS // 2
    large = max_exact + (jnp.log(jnp.maximum(n, 1).astype(jnp.float32) / max_exact)
                         / math.log(REL_MAX_DIST / max_exact)
                         * (REL_BUCKETS - max_exact)).astype(jnp.int32)
    large = jnp.minimum(large, REL_BUCKETS - 1)
    return jnp.where(n < max_exact, n, large)


def moba_attention(q, k, v, rel_bias):
    bsz, n_heads, seq, hd = q.shape
    nb = -(-seq // MOBA_BLOCK)
    pad = nb * MOBA_BLOCK - seq
    k_blk = jnp.pad(k, ((0, 0), (0, 0), (0, pad), (0, 0))).reshape(bsz, n_heads, nb, MOBA_BLOCK, hd)
    v_blk = jnp.pad(v, ((0, 0), (0, 0), (0, pad), (0, 0))).reshape(bsz, n_heads, nb, MOBA_BLOCK, hd)
    k_mean = jnp.mean(k_blk.astype(jnp.float32), axis=3).astype(q.dtype)
    topk = min(MOBA_TOPK, nb)
    n_chunks = seq // MOBA_QCHUNK
    q_chunks = q.reshape(bsz, n_heads, n_chunks, MOBA_QCHUNK, hd).transpose(2, 0, 1, 3, 4)
    bias_tab = rel_bias.T
    head_idx = jnp.arange(n_heads)[None, :, None, None, None]
    blk_ids = jnp.arange(nb)
    offs = jnp.arange(MOBA_BLOCK)
    gather_blocks = jax.vmap(jax.vmap(lambda blocks, ix: blocks[ix]))
    scale = hd ** -0.5

    def one_chunk(args):
        c, qc = args
        start = c * MOBA_QCHUNK
        qpos = start + jnp.arange(MOBA_QCHUNK)
        own = start // MOBA_BLOCK
        gate = jnp.einsum('bhqd,bhnd->bhqn', qc, k_mean).astype(jnp.float32)
        gate = jnp.where(blk_ids < own, gate, -jnp.inf)
        _, idx = lax.top_k(gate, topk)
        valid = jnp.arange(topk) < own
        k_sel = gather_blocks(k_blk, idx)
        v_sel = gather_blocks(v_blk, idx)
        kpos_sel = idx[..., None] * MOBA_BLOCK + offs
        s_sel = jnp.einsum('bhqd,bhqjud->bhqju', qc, k_sel).astype(jnp.float32) * scale
        s_sel = s_sel + bias_tab[head_idx, t5_bucket(qpos[:, None, None] - kpos_sel)]
        s_sel = jnp.where(valid[:, None], s_sel, -jnp.inf).reshape(bsz, n_heads, MOBA_QCHUNK, topk * MOBA_BLOCK)
        k_own = lax.dynamic_index_in_dim(k_blk, own, axis=2, keepdims=False)
        v_own = lax.dynamic_index_in_dim(v_blk, own, axis=2, keepdims=False)
        kpos_own = own * MOBA_BLOCK + offs
        s_own = jnp.einsum('bhqd,bhud->bhqu', qc, k_own).astype(jnp.float32) * scale
        s_own = s_own + bias_tab[:, t5_bucket(qpos[:, None] - kpos_own[None, :])]
        s_own = jnp.where(kpos_own[None, :] <= qpos[:, None], s_own, -jnp.inf)
        p = jax.nn.softmax(jnp.concatenate([s_sel, s_own], axis=-1), axis=-1).astype(v.dtype)
        p_sel = p[..., :topk * MOBA_BLOCK].reshape(bsz, n_heads, MOBA_QCHUNK, topk, MOBA_BLOCK)
        p_own = p[..., topk * MOBA_BLOCK:]
        return (jnp.einsum('bhqju,bhqjud->bhqd', p_sel, v_sel)
                + jnp.einsum('bhqu,bhud->bhqd', p_own, v_own))

    out = lax.map(one_chunk, (jnp.arange(n_chunks), q_chunks))
    return out.transpose(1, 2, 0, 3, 4).reshape(bsz, n_heads, seq, hd)


def diff_attention(q1, q2, k1, k2, v, rel_bias, lam):
    bsz, n_heads, seq, hd = q1.shape
    nblk = seq // DENSE_QBLOCK
    scale = hd ** -0.5
    bias_tab = rel_bias.T.reshape(DIFF_HEADS, 2, REL_BUCKETS)
    kpos = jnp.arange(seq)
    qb1 = q1.reshape(bsz, n_heads, nblk, DENSE_QBLOCK, hd).transpose(2, 0, 1, 3, 4)
    qb2 = q2.reshape(bsz, n_heads, nblk, DENSE_QBLOCK, hd).transpose(2, 0, 1, 3, 4)

    def one_block(args):
        i, a1q, a2q = args
        qpos = i * DENSE_QBLOCK + jnp.arange(DENSE_QBLOCK)
        bias = bias_tab[:, :, t5_bucket(qpos[:, None] - kpos[None, :])]
        causal = kpos[None, :] <= qpos[:, None]
        s1 = jnp.einsum('bhqd,bhkd->bhqk', a1q, k1).astype(jnp.float32) * scale + bias[:, 0]
        s2 = jnp.einsum('bhqd,bhkd->bhqk', a2q, k2).astype(jnp.float32) * scale + bias[:, 1]
        a1 = jax.nn.softmax(jnp.where(causal, s1, -jnp.inf), axis=-1)
        a2 = jax.nn.softmax(jnp.where(causal, s2, -jnp.inf), axis=-1)
        a = (a1 - lam * a2).astype(v.dtype)
        return jnp.einsum('bhqk,bhkv->bhqv', a, v)

    out = lax.map(one_block, (jnp.arange(nblk), qb1, qb2))
    return out.transpose(1, 2, 0, 3, 4).reshape(bsz, n_heads, seq, -1)


def memory_attention(qm, mem, g_mem, w_mkv, gk):
    kv = rmsnorm(mem, g_mem) @ w_mkv
    km = rmsnorm(split_heads(kv[..., :MEM_WIDTH], MEM_HEADS), gk)
    vm = split_heads(kv[..., MEM_WIDTH:], MEM_HEADS)
    s = jnp.einsum('bhqd,bhmd->bhqm', qm, km).astype(jnp.float32) * (HEAD_DIM ** -0.5)
    p = jax.nn.softmax(s, axis=-1).astype(vm.dtype)
    return merge_heads(jnp.einsum('bhqm,bhmd->bhqd', p, vm))


def hier_moe(x, w_rg, b_rg, w_re, b_re, w_gate, w_up, w_down):
    bsz, seq, dm = x.shape
    xt = x.reshape(-1, dm)
    n_tok = xt.shape[0]
    g_logits = (xt @ w_rg).astype(jnp.float32) + b_rg.astype(jnp.float32)
    g_prob = jax.nn.softmax(g_logits, axis=-1)
    g_idx = jnp.argmax(g_logits, axis=-1)
    p_g = jnp.take_along_axis(g_prob, g_idx[:, None], axis=-1)[:, 0]
    e_logits = ((xt @ w_re).astype(jnp.float32) + b_re.astype(jnp.float32)).reshape(n_tok, N_GROUPS, EXPERTS_PER_GROUP)
    e_logits_g = jnp.take_along_axis(e_logits, g_idx[:, None, None], axis=1)[:, 0]
    top_v, top_i = lax.top_k(e_logits_g, TOPK_IN_GROUP)
    gates = p_g[:, None] * jax.nn.softmax(top_v, axis=-1)
    expert = g_idx[:, None] * EXPERTS_PER_GROUP + top_i
    n_asg = n_tok * TOPK_IN_GROUP
    e_flat = expert.reshape(-1)
    t_flat = jnp.repeat(jnp.arange(n_tok), TOPK_IN_GROUP)
    w_flat = gates.reshape(-1)
    order = jnp.argsort(e_flat)
    e_s, t_s, w_s = e_flat[order], t_flat[order], w_flat[order]
    counts = jnp.bincount(e_flat, length=N_EXPERTS)
    starts = jnp.cumsum(counts) - counts
    padded = (counts + MOE_BLOCK - 1) // MOE_BLOCK * MOE_BLOCK
    pends = jnp.cumsum(padded)
    pstarts = pends - padded
    dest = pstarts[e_s] + (jnp.arange(n_asg) - starts[e_s])
    n_blocks = -(-n_asg // MOE_BLOCK) + N_EXPERTS
    n_rows = n_blocks * MOE_BLOCK
    buf_tok = jnp.zeros((n_rows,), jnp.int32).at[dest].set(t_s)
    buf_w = jnp.zeros((n_rows,), jnp.float32).at[dest].set(w_s)
    blk_e = jnp.minimum(jnp.searchsorted(pends, jnp.arange(n_blocks) * MOE_BLOCK, side='right'), N_EXPERTS - 1)
    xs = xt[buf_tok].reshape(n_blocks, MOE_BLOCK, dm)

    def expert_block(args):
        xb, e = args
        hdn = jax.nn.silu(xb @ w_gate[e]) * (xb @ w_up[e])
        return hdn @ w_down[e]

    ys = lax.map(expert_block, (xs, blk_e)).reshape(n_rows, dm)
    out = jnp.zeros((n_tok, dm), jnp.float32).at[buf_tok].add(ys.astype(jnp.float32) * buf_w[:, None])
    return out.astype(x.dtype).reshape(bsz, seq, dm)


def setup_inputs(seed: int = 0) -> dict:
    key = jax.random.key(seed)
    ks = iter(jax.random.split(key, 40))
    f32 = jnp.float32
    sd = D_MODEL ** -0.5

    def nrm(shape, scale):
        return jax.random.normal(next(ks), shape, f32) * scale

    def gain(shape):
        return 1.0 + nrm(shape, 0.02)

    kv_width = 2 * DIFF_HEADS * HEAD_DIM + DIFF_HEADS * DIFF_VDIM
    return {
        'x': nrm((BATCH, SEQ, D_MODEL), 1.0),
        'mem': nrm((BATCH, N_MEM, D_MODEL), 1.0),
        'rel_bias': nrm((REL_BUCKETS, N_BIAS_MAPS), 0.5),
        'attn_norm_g': gain((DEPTH, D_MODEL)),
        'ffn_norm_g': gain((DEPTH, D_MODEL)),
        'mem_norm_g': gain((DEPTH, D_MODEL)),
        'w_mem_kv': nrm((DEPTH, D_MODEL, 2 * MEM_WIDTH), sd),
        'q_norm_mem': gain((DEPTH, HEAD_DIM)),
        'k_norm_mem': gain((DEPTH, HEAD_DIM)),
        'w_out': nrm((DEPTH, D_MODEL, D_MODEL), sd),
        'w_in_a': nrm((N_A, D_MODEL, 3 * MIX_WIDTH + MEM_WIDTH), sd),
        'q_norm_a': gain((N_A, HEAD_DIM)),
        'k_norm_a': gain((N_A, HEAD_DIM)),
        'kv_norm_g': gain((D_MODEL,)),
        'w_kv_b': nrm((D_MODEL, kv_width), sd),
        'k_norm_b': gain((2, HEAD_DIM)),
        'w_in_b': nrm((N_B, D_MODEL, 2 * DIFF_HEADS * HEAD_DIM + MEM_WIDTH), sd),
        'q_norm_b': gain((N_B, 2, HEAD_DIM)),
        'lambda_b': nrm((N_B, 4, HEAD_DIM), 0.1),
        'subln_g': gain((N_B, DIFF_VDIM)),
        'w_router_group': nrm((DEPTH, D_MODEL, N_GROUPS), sd),
        'b_router_group': nrm((DEPTH, N_GROUPS), 0.01),
        'w_router_expert': nrm((DEPTH, D_MODEL, N_EXPERTS), sd),
        'b_router_expert': nrm((DEPTH, N_EXPERTS), 0.01),
        'w_gate': nrm((DEPTH, N_EXPERTS, D_MODEL, EXPERT_FF), sd),
        'w_up': nrm((DEPTH, N_EXPERTS, D_MODEL, EXPERT_FF), sd),
        'w_down': nrm((DEPTH, N_EXPERTS, EXPERT_FF, D_MODEL), EXPERT_FF ** -0.5),
    }


def reference(x, mem, rel_bias, attn_norm_g, ffn_norm_g, mem_norm_g, w_mem_kv, q_norm_mem, k_norm_mem,
              w_out, w_in_a, q_norm_a, k_norm_a, kv_norm_g, w_kv_b, k_norm_b, w_in_b, q_norm_b,
              lambda_b, subln_g, w_router_group, b_router_group, w_router_expert, b_router_expert,
              w_gate, w_up, w_down):
    qk_w = DIFF_HEADS * HEAD_DIM
    h = x
    k1s = k2s = vs = None
    for l in range(DEPTH):
        xn = rmsnorm(h, attn_norm_g[l])
        if l < N_A:
            proj = xn @ w_in_a[l]
            q = rmsnorm(split_heads(proj[..., :MIX_WIDTH], MOBA_HEADS), q_norm_a[l])
            k = rmsnorm(split_heads(proj[..., MIX_WIDTH:2 * MIX_WIDTH], MOBA_HEADS), k_norm_a[l])
            v = split_heads(proj[..., 2 * MIX_WIDTH:3 * MIX_WIDTH], MOBA_HEADS)
            qm = proj[..., 3 * MIX_WIDTH:]
            o_mix = merge_heads(moba_attention(q, k, v, rel_bias))
        else:
            j = l - N_A
            proj = xn @ w_in_b[j]
            q1 = rmsnorm(split_heads(proj[..., :qk_w], DIFF_HEADS), q_norm_b[j, 0])
            q2 = rmsnorm(split_heads(proj[..., qk_w:2 * qk_w], DIFF_HEADS), q_norm_b[j, 1])
            qm = proj[..., 2 * qk_w:]
            lam_init = 0.8 - 0.6 * math.exp(-0.3 * l)
            lam_vec = lambda_b[j].astype(jnp.float32)
            lam = (jnp.exp(jnp.sum(lam_vec[0] * lam_vec[1])) - jnp.exp(jnp.sum(lam_vec[2] * lam_vec[3]))
                   + lam_init)
            o = diff_attention(q1, q2, k1s, k2s, vs, rel_bias, lam)
            o_mix = merge_heads(rmsnorm(o, subln_g[j]) * (1.0 - lam_init))
        qm = rmsnorm(split_heads(qm, MEM_HEADS), q_norm_mem[l])
        o_mem = memory_attention(qm, mem, mem_norm_g[l], w_mem_kv[l], k_norm_mem[l])
        h = h + jnp.concatenate([o_mix, o_mem], axis=-1) @ w_out[l]
        h = h + hier_moe(rmsnorm(h, ffn_norm_g[l]), w_router_group[l], b_router_group[l],
                         w_router_expert[l], b_router_expert[l], w_gate[l], w_up[l], w_down[l])
        if l == N_A - 1:
            kv = rmsnorm(h, kv_norm_g) @ w_kv_b
            k1s = rmsnorm(split_heads(kv[..., :qk_w], DIFF_HEADS), k_norm_b[0])
            k2s = rmsnorm(split_heads(kv[..., qk_w:2 * qk_w], DIFF_HEADS), k_norm_b[1])
            vs = split_heads(kv[..., 2 * qk_w:], DIFF_HEADS)
    return h
```

```python
import functools
import math

import jax
import jax.numpy as jnp
from jax import lax
from jax.experimental import pallas as pl
from jax.experimental.pallas import tpu as pltpu

F32 = jnp.float32
BF16 = jnp.bfloat16

HEAD_DIM = 128
LANES = 128
N_MEM_HEADS = 4
MOBA_BLOCK = 256
MOBA_TOPK = 3
DIFF_VDIM = 2 * HEAD_DIM
REL_BUCKETS = 32
REL_MAX_DIST = 128
N_GROUPS = 8
EXPERTS_PER_GROUP = 8
N_EXPERTS = N_GROUPS * EXPERTS_PER_GROUP
MOE_BLOCK = 256
RMS_EPS = 1e-6
ATTN_SCALE = HEAD_DIM ** -0.5
MASKED = -1e30
VMEM_LIMIT = 56 * 1024 * 1024


def _cparams(*sem):
    return pltpu.CompilerParams(dimension_semantics=sem, vmem_limit_bytes=VMEM_LIMIT)


def _dot_t(a, b):
    return lax.dot_general(a, b, (((1,), (1,)), ((), ())), preferred_element_type=F32)


def _dot(a, b):
    return jnp.dot(a, b, preferred_element_type=F32)


def _norm_proj_kernel(mode_ref, h_ref, g_ref, w_ref, gain_ref, o_ref, xn_ref):
    j = pl.program_id(1)

    @pl.when(j == 0)
    def _():
        x = h_ref[...]
        ms = jnp.mean(x * x, axis=-1, keepdims=True)
        xn_ref[...] = (x * lax.rsqrt(ms + RMS_EPS) * g_ref[...]).astype(BF16)

    acc = _dot(xn_ref[...], w_ref[...])
    n_heads = acc.shape[1] // HEAD_DIM

    @pl.when(mode_ref[j] == 1)
    def _():
        for hh in range(n_heads):
            sl = slice(hh * HEAD_DIM, (hh + 1) * HEAD_DIM)
            a = acc[:, sl]
            ms = jnp.mean(a * a, axis=-1, keepdims=True)
            o_ref[:, sl] = (a * lax.rsqrt(ms + RMS_EPS) * gain_ref[:, sl]).astype(o_ref.dtype)

    @pl.when(mode_ref[j] != 1)
    def _():
        o_ref[...] = acc.astype(o_ref.dtype)


def _norm_proj(h, g, w, gains, modes, *, tn=512):
    n, d = h.shape
    n_out = w.shape[1]
    tm = min(512, n)
    grid = (n // tm, n_out // tn)
    return pl.pallas_call(
        _norm_proj_kernel,
        grid_spec=pltpu.PrefetchScalarGridSpec(
            num_scalar_prefetch=1,
            grid=grid,
            in_specs=[
                pl.BlockSpec((tm, d), lambda i, j, m: (i, 0)),
                pl.BlockSpec((1, d), lambda i, j, m: (0, 0)),
                pl.BlockSpec((d, tn), lambda i, j, m: (0, j)),
                pl.BlockSpec((1, tn), lambda i, j, m: (0, j)),
            ],
            out_specs=pl.BlockSpec((tm, tn), lambda i, j, m: (i, j)),
            scratch_shapes=[pltpu.VMEM((tm, d), BF16)],
        ),
        out_shape=jax.ShapeDtypeStruct((n, n_out), BF16),
        compiler_params=_cparams("parallel", "arbitrary"),
        name="norm_proj",
    )(modes, h, g.reshape(1, d).astype(F32), w, gains.reshape(1, n_out).astype(F32))


def _t5_bucket(dist):
    n = jnp.maximum(dist, 0)
    max_exact = REL_BUCKETS // 2
    large = max_exact + (jnp.log(jnp.maximum(n, 1).astype(F32) / max_exact)
                         / math.log(REL_MAX_DIST / max_exact)
                         * (REL_BUCKETS - max_exact)).astype(jnp.int32)
    large = jnp.minimum(large, REL_BUCKETS - 1)
    return jnp.where(n < max_exact, n, large)


def _near_bias_tiles(rel_bias):
    t = MOBA_BLOCK
    tab = rel_bias.T.astype(F32)
    by_dist = tab[:, _t5_bucket(jnp.arange(2 * t))]
    far = tab[:, REL_BUCKETS - 1][:, None, None]
    r = jnp.arange(t)[:, None]
    c = jnp.arange(t)[None, :]
    d0 = r - c
    own = jnp.where(d0 >= 0, by_dist[:, jnp.maximum(d0, 0)] - far, MASKED)
    prev = by_dist[:, d0 + t] - far
    return jnp.stack([own, prev], axis=1)


def _online_update(carry, s, v):
    m, l, acc = carry
    m_new = jnp.maximum(m, jnp.max(s, axis=-1, keepdims=True))
    alpha = jnp.exp(m - m_new)
    p = jnp.exp(s - m_new)
    l = alpha * l + jnp.sum(p, axis=-1, keepdims=True)
    acc = alpha * acc + _dot(p.astype(BF16), v)
    return m_new, l, acc


def _moba_kernel(q_ref, k_ref, v_ref, bias_ref, o_ref, kmean_ref, *, nb):
    i = pl.program_id(2)
    t = MOBA_BLOCK

    @pl.when(i == 0)
    def _():
        kmean_ref[...] = jnp.zeros_like(kmean_ref)
        for jb in range(nb):
            kb = k_ref[0, jb * t:(jb + 1) * t, :].astype(F32)
            kmean_ref[jb:jb + 1, :] = jnp.mean(kb, axis=0, keepdims=True)

    q = q_ref[0]

    km = kmean_ref[...]
    km_hi = km.astype(BF16)
    km_lo = (km - km_hi.astype(F32)).astype(BF16)
    gate = _dot_t(q, km_hi) + _dot_t(q, km_lo)
    lane = lax.broadcasted_iota(jnp.int32, (t, LANES), 1).astype(F32)
    neg = jnp.float32(-3e38)
    gcur = jnp.where(lane < i.astype(F32), gate, neg)
    sel = jnp.zeros((t, LANES), jnp.bool_)
    for _ in range(MOBA_TOPK):
        mx = jnp.max(gcur, axis=-1, keepdims=True)
        idx = jnp.min(jnp.where(gcur == mx, lane, float(LANES)), axis=-1, keepdims=True)
        hit = lane == idx
        sel = sel | (hit & (mx > 0.5 * neg))
        gcur = jnp.where(hit, neg, gcur)
    pen = jnp.where(sel, 0.0, MASKED).astype(BF16)
    q_aug = jnp.concatenate([q, pen], axis=1)

    def k_aug(j):
        start = pl.multiple_of(j * t, t)
        kb = k_ref[0, pl.ds(start, t), :]
        onehot = (lax.broadcasted_iota(jnp.int32, (t, LANES), 1) == j).astype(BF16)
        return jnp.concatenate([kb, onehot], axis=1)

    def v_blk(j):
        start = pl.multiple_of(j * t, t)
        return v_ref[0, pl.ds(start, t), :]

    start_i = pl.multiple_of(i * t, t)
    s = _dot_t(q, k_ref[0, pl.ds(start_i, t), :]) + bias_ref[0, 0]
    m = jnp.max(s, axis=-1, keepdims=True)
    p = jnp.exp(s - m)
    carry = (m, jnp.sum(p, axis=-1, keepdims=True), _dot(p.astype(BF16), v_blk(i)))

    def prev_block(c):
        s = _dot_t(q_aug, k_aug(i - 1)) + bias_ref[0, 1]
        return _online_update(c, s, v_blk(i - 1))

    carry = lax.cond(i >= 1, prev_block, lambda c: c, carry)

    def far_block(j, c):
        return _online_update(c, _dot_t(q_aug, k_aug(j)), v_blk(j))

    m, l, acc = lax.fori_loop(0, jnp.maximum(i - 1, 0), far_block, carry)
    o_ref[0] = (acc / l).astype(o_ref.dtype)


def _moba_attention(proj, bias_tiles, *, n_heads, q_off, k_off, v_off):
    b, s, _ = proj.shape
    t = MOBA_BLOCK
    nb = s // t
    d = HEAD_DIM
    return pl.pallas_call(
        functools.partial(_moba_kernel, nb=nb),
        grid=(b, n_heads, nb),
        in_specs=[
            pl.BlockSpec((1, t, d), lambda bb, h, i: (bb, i, q_off + h)),
            pl.BlockSpec((1, s, d), lambda bb, h, i: (bb, 0, k_off + h)),
            pl.BlockSpec((1, s, d), lambda bb, h, i: (bb, 0, v_off + h)),
            pl.BlockSpec((1, 2, t, t), lambda bb, h, i: (h, 0, 0, 0)),
        ],
        out_specs=pl.BlockSpec((1, t, d), lambda bb, h, i: (bb, i, h)),
        out_shape=jax.ShapeDtypeStruct((b, s, n_heads * d), BF16),
        scratch_shapes=[pltpu.VMEM((LANES, d), F32)],
        compiler_params=_cparams("parallel", "parallel", "arbitrary"),
        name="moba_attention",
    )(proj, proj, proj, bias_tiles)


def _diff_kernel(lam_ref, q1_ref, q2_ref, k1_ref, k2_ref, v_ref, bias_ref, g_ref, o_ref,
                 *, out_scale):
    i = pl.program_id(2)
    t = MOBA_BLOCK
    qs = (q1_ref[0], q2_ref[0])
    ks = (k1_ref, k2_ref)

    def blk(ref, j):
        start = pl.multiple_of(j * t, t)
        return ref[0, pl.ds(start, t), :]

    carries = []
    for mi in range(2):
        s = _dot_t(qs[mi], blk(ks[mi], i)) + bias_ref[0, mi, 0]
        m = jnp.max(s, axis=-1, keepdims=True)
        p = jnp.exp(s - m)
        carries.append((m, jnp.sum(p, axis=-1, keepdims=True), _dot(p.astype(BF16), blk(v_ref, i))))
    carry = tuple(carries)

    def prev_block(c):
        v = blk(v_ref, i - 1)
        return tuple(
            _online_update(c[mi], _dot_t(qs[mi], blk(ks[mi], i - 1)) + bias_ref[0, mi, 1], v)
            for mi in range(2))

    carry = lax.cond(i >= 1, prev_block, lambda c: c, carry)

    def far_block(j, c):
        v = blk(v_ref, j)
        return tuple(
            _online_update(c[mi], _dot_t(qs[mi], blk(ks[mi], j)), v) for mi in range(2))

    (_, l1, acc1), (_, l2, acc2) = lax.fori_loop(0, jnp.maximum(i - 1, 0), far_block, carry)
    o = acc1 / l1 - lam_ref[0] * (acc2 / l2)
    ms = jnp.mean(o * o, axis=-1, keepdims=True)
    o_ref[0] = (o * lax.rsqrt(ms + RMS_EPS) * g_ref[...] * out_scale).astype(o_ref.dtype)


def _diff_attention(proj, kv, bias_tiles, lam, subln_g, *, n_heads, out_scale):
    b, s, _ = proj.shape
    t = MOBA_BLOCK
    d = HEAD_DIM
    dv = DIFF_VDIM
    return pl.pallas_call(
        functools.partial(_diff_kernel, out_scale=out_scale),
        grid=(b, n_heads, s // t),
        in_specs=[
            pl.BlockSpec(memory_space=pltpu.SMEM),
            pl.BlockSpec((1, t, d), lambda bb, h, i: (bb, i, h)),
            pl.BlockSpec((1, t, d), lambda bb, h, i: (bb, i, n_heads + h)),
            pl.BlockSpec((1, s, d), lambda bb, h, i: (bb, 0, h)),
            pl.BlockSpec((1, s, d), lambda bb, h, i: (bb, 0, n_heads + h)),
            pl.BlockSpec((1, s, dv), lambda bb, h, i: (bb, 0, n_heads + h)),
            pl.BlockSpec((1, 2, 2, t, t), lambda bb, h, i: (h, 0, 0, 0, 0)),
            pl.BlockSpec((1, dv), lambda bb, h, i: (0, 0)),
        ],
        out_specs=pl.BlockSpec((1, t, dv), lambda bb, h, i: (bb, i, h)),
        out_shape=jax.ShapeDtypeStruct((b, s, n_heads * dv), BF16),
        compiler_params=_cparams("parallel", "parallel", "arbitrary"),
        name="diff_attention",
    )(lam.reshape(1).astype(F32), proj, proj, kv, kv, kv, bias_tiles,
      subln_g.reshape(1, dv).astype(F32))


def _mem_attn_kernel(q_ref, k_ref, v_ref, o_ref):
    s = _dot_t(q_ref[0], k_ref[0])
    m = jnp.max(s, axis=-1, keepdims=True)
    p = jnp.exp(s - m)
    l = jnp.sum(p, axis=-1, keepdims=True)
    o_ref[0] = (_dot(p.astype(BF16), v_ref[0]) / l).astype(o_ref.dtype)


def _mem_attention(proj, kvm, *, q_off):
    b, s, _ = proj.shape
    n_mem = kvm.shape[1]
    d = HEAD_DIM
    tq = min(1024, s)
    return pl.pallas_call(
        _mem_attn_kernel,
        grid=(b, N_MEM_HEADS, s // tq),
        in_specs=[
            pl.BlockSpec((1, tq, d), lambda bb, h, i: (bb, i, q_off + h)),
            pl.BlockSpec((1, n_mem, d), lambda bb, h, i: (bb, 0, h)),
            pl.BlockSpec((1, n_mem, d), lambda bb, h, i: (bb, 0, N_MEM_HEADS + h)),
        ],
        out_specs=pl.BlockSpec((1, tq, d), lambda bb, h, i: (bb, i, h)),
        out_shape=jax.ShapeDtypeStruct((b, s, N_MEM_HEADS * d), BF16),
        compiler_params=_cparams("parallel", "parallel", "arbitrary"),
        name="mem_attention",
    )(proj, kvm, kvm)


def _out_proj_kernel(a_ref, b_ref, wa_ref, wb_ref, h_ref, o_ref):
    o_ref[...] = h_ref[...] + _dot(a_ref[...], wa_ref[...]) + _dot(b_ref[...], wb_ref[...])


def _out_proj(o_mix, o_mem, w_out, h, *, tm=512, tn=512):
    n, d = h.shape
    wa = o_mix.shape[1]
    wb = o_mem.shape[1]
    return pl.pallas_call(
        _out_proj_kernel,
        grid=(n // tm, d // tn),
        in_specs=[
            pl.BlockSpec((tm, wa), lambda i, j: (i, 0)),
            pl.BlockSpec((tm, wb), lambda i, j: (i, 0)),
            pl.BlockSpec((wa, tn), lambda i, j: (0, j)),
            pl.BlockSpec((wb, tn), lambda i, j: (0, j)),
            pl.BlockSpec((tm, tn), lambda i, j: (i, j)),
        ],
        out_specs=pl.BlockSpec((tm, tn), lambda i, j: (i, j)),
        out_shape=jax.ShapeDtypeStruct((n, d), F32),
        compiler_params=_cparams("parallel", "arbitrary"),
        name="out_proj",
    )(o_mix, o_mem, w_out[:wa], w_out[wa:], h)


def _router_kernel(h_ref, g_ref, wh_ref, wl_ref, b_ref, xn_ref, info_ref):
    x = h_ref[...]
    ms = jnp.mean(x * x, axis=-1, keepdims=True)
    xn = x * lax.rsqrt(ms + RMS_EPS) * g_ref[...]
    xn_ref[...] = xn
    xh = xn.astype(BF16)
    xl = (xn - xh.astype(F32)).astype(BF16)
    logits = _dot(xh, wh_ref[...]) + _dot(xl, wh_ref[...]) + _dot(xh, wl_ref[...]) + b_ref[...]

    tm = logits.shape[0]
    lane_i = lax.broadcasted_iota(jnp.int32, (tm, LANES), 1)
    lane = lane_i.astype(F32)
    neg = jnp.float32(-3e38)
    big = float(LANES)

    def top1(vals):
        mx = jnp.max(vals, axis=-1, keepdims=True)
        idx = jnp.min(jnp.where(vals == mx, lane, big), axis=-1, keepdims=True)
        return mx, idx

    is_group = lane_i < N_GROUPS
    gl = jnp.where(is_group, logits, neg)
    gmax, gidx = top1(gl)
    p_g = 1.0 / jnp.sum(jnp.where(is_group, jnp.exp(logits - gmax), 0.0), axis=-1, keepdims=True)

    e_lo = N_GROUPS + gidx * EXPERTS_PER_GROUP
    member = (lane >= e_lo) & (lane < e_lo + EXPERTS_PER_GROUP)
    el = jnp.where(member, logits, neg)
    v1, i1 = top1(el)
    v2, i2 = top1(jnp.where(lane == i1, neg, el))
    w2 = jnp.exp(v2 - v1)
    w1 = 1.0 / (1.0 + w2)
    info = jnp.where(lane_i == 0, i1 - N_GROUPS, 0.0)
    info = jnp.where(lane_i == 1, i2 - N_GROUPS, info)
    info = jnp.where(lane_i == 2, p_g * w1, info)
    info = jnp.where(lane_i == 3, p_g * (w2 * w1), info)
    info_ref[...] = info


def _router(h, g, w_rg, b_rg, w_re, b_re, *, tm=512):
    n, d = h.shape
    pad = LANES - N_GROUPS - N_EXPERTS
    w = jnp.concatenate([w_rg, w_re, jnp.zeros((d, pad), F32)], axis=1).astype(F32)
    bias = jnp.concatenate([b_rg, b_re, jnp.zeros((pad,), F32)]).astype(F32).reshape(1, LANES)
    w_hi = w.astype(BF16)
    w_lo = (w - w_hi.astype(F32)).astype(BF16)
    return pl.pallas_call(
        _router_kernel,
        grid=(n // tm,),
        in_specs=[
            pl.BlockSpec((tm, d), lambda i: (i, 0)),
            pl.BlockSpec((1, d), lambda i: (0, 0)),
            pl.BlockSpec((d, LANES), lambda i: (0, 0)),
            pl.BlockSpec((d, LANES), lambda i: (0, 0)),
            pl.BlockSpec((1, LANES), lambda i: (0, 0)),
        ],
        out_specs=[
            pl.BlockSpec((tm, d), lambda i: (i, 0)),
            pl.BlockSpec((tm, LANES), lambda i: (i, 0)),
        ],
        out_shape=[jax.ShapeDtypeStruct((n, d), F32), jax.ShapeDtypeStruct((n, LANES), F32)],
        compiler_params=_cparams("parallel"),
        name="moe_router",
    )(h, g.reshape(1, d).astype(F32), w_hi, w_lo, bias)


def _dispatch_plan(info, n_tok):
    n_asg = 2 * n_tok
    e_flat = info[:, :2].astype(jnp.int32).reshape(-1)
    order = jnp.argsort(e_flat).astype(jnp.int32)
    e_s = e_flat[order]
    counts = jnp.zeros((N_EXPERTS,), jnp.int32).at[e_flat].add(1)
    starts = jnp.cumsum(counts) - counts
    padded = (counts + MOE_BLOCK - 1) // MOE_BLOCK * MOE_BLOCK
    pends = jnp.cumsum(padded)
    pstarts = pends - padded
    dest = pstarts[e_s] + (jnp.arange(n_asg, dtype=jnp.int32) - starts[e_s])
    n_blocks = -(-n_asg // MOE_BLOCK) + N_EXPERTS
    n_rows = n_blocks * MOE_BLOCK
    tok = order // 2
    slot = order - 2 * tok
    buf_tok = jnp.zeros((n_rows,), jnp.int32).at[dest].set(tok)
    buf_dst = jnp.full((n_rows,), -1, jnp.int32).at[dest].set(slot * n_tok + tok)
    blk_e = jnp.minimum(
        jnp.searchsorted(pends, jnp.arange(n_blocks, dtype=jnp.int32) * MOE_BLOCK, side='right'),
        N_EXPERTS - 1).astype(jnp.int32)
    n_used = (pends[-1] // MOE_BLOCK).astype(jnp.int32).reshape(1)
    return (buf_tok.reshape(n_blocks, 1, MOE_BLOCK), buf_dst.reshape(n_blocks, 1, MOE_BLOCK),
            blk_e, n_used)


def _moe_kernel(blk_e_ref, n_used_ref, tok_ref, dst_ref, x_hbm, wg_ref, wu_ref, wd_ref, y_hbm,
                xbuf, ybuf, wg_bf, wu_bf, wd_bf, sem_in, sem_out):
    b = pl.program_id(0)
    rows = MOE_BLOCK

    def gather_copy(r, t):
        return pltpu.make_async_copy(x_hbm.at[pl.ds(t, 1), :], xbuf.at[pl.ds(r, 1), :], sem_in)

    def scatter_copy(r, d):
        return pltpu.make_async_copy(ybuf.at[pl.ds(r, 1), :], y_hbm.at[pl.ds(d, 1), :], sem_out)

    @pl.when(b < n_used_ref[0])
    def _():
        def start_gather(r, c):
            gather_copy(r, tok_ref[0, 0, r]).start()
            return c
        lax.fori_loop(0, rows, start_gather, 0)

        e = blk_e_ref[b]
        e_prev = blk_e_ref[jnp.maximum(b - 1, 0)]

        @pl.when((b == 0) | (e != e_prev))
        def _():
            wg_bf[...] = wg_ref[0].astype(BF16)
            wu_bf[...] = wu_ref[0].astype(BF16)
            wd_bf[...] = wd_ref[0].astype(BF16)

        def wait_gather(r, c):
            gather_copy(r, 0).wait()
            return c
        lax.fori_loop(0, rows, wait_gather, 0)

        x = xbuf[...].astype(BF16)
        hg = _dot(x, wg_bf[...])
        hu = _dot(x, wu_bf[...])
        hdn = (hg * jax.nn.sigmoid(hg)) * hu
        ybuf[...] = _dot(hdn.astype(BF16), wd_bf[...])

        def start_scatter(r, c):
            d = dst_ref[0, 0, r]

            @pl.when(d >= 0)
            def _():
                scatter_copy(r, d).start()
            return c
        lax.fori_loop(0, rows, start_scatter, 0)

        def wait_scatter(r, c):
            d = dst_ref[0, 0, r]

            @pl.when(d >= 0)
            def _():
                scatter_copy(r, d).wait()
            return c
        lax.fori_loop(0, rows, wait_scatter, 0)


def _moe_experts(xn, plan, w_gate, w_up, w_down):
    buf_tok, buf_dst, blk_e, n_used = plan
    n_tok, d = xn.shape
    ff = w_gate.shape[2]
    n_blocks = buf_tok.shape[0]
    smem_blk = pl.BlockSpec((1, 1, MOE_BLOCK), lambda b, be, nu: (b, 0, 0), memory_space=pltpu.SMEM)
    return pl.pallas_call(
        _moe_kernel,
        grid_spec=pltpu.PrefetchScalarGridSpec(
            num_scalar_prefetch=2,
            grid=(n_blocks,),
            in_specs=[
                smem_blk,
                smem_blk,
                pl.BlockSpec(memory_space=pl.ANY),
                pl.BlockSpec((1, d, ff), lambda b, be, nu: (be[b], 0, 0)),
                pl.BlockSpec((1, d, ff), lambda b, be, nu: (be[b], 0, 0)),
                pl.BlockSpec((1, ff, d), lambda b, be, nu: (be[b], 0, 0)),
            ],
            out_specs=pl.BlockSpec(memory_space=pl.ANY),
            scratch_shapes=[
                pltpu.VMEM((MOE_BLOCK, d), F32),
                pltpu.VMEM((MOE_BLOCK, d), F32),
                pltpu.VMEM((d, ff), BF16),
                pltpu.VMEM((d, ff), BF16),
                pltpu.VMEM((ff, d), BF16),
                pltpu.SemaphoreType.DMA,
                pltpu.SemaphoreType.DMA,
            ],
        ),
        out_shape=jax.ShapeDtypeStruct((2 * n_tok, d), F32),
        compiler_params=_cparams("arbitrary"),
        name="moe_experts",
    )(blk_e, n_used, buf_tok, buf_dst, xn, w_gate, w_up, w_down)


def _combine_kernel(h_ref, y0_ref, y1_ref, info_ref, o_ref):
    info = info_ref[...]
    o_ref[...] = h_ref[...] + (info[:, 2:3] * y0_ref[...] + info[:, 3:4] * y1_ref[...])


def _combine(h, y2, info, *, tm=512):
    n, d = h.shape
    nt = n // tm
    return pl.pallas_call(
        _combine_kernel,
        grid=(nt,),
        in_specs=[
            pl.BlockSpec((tm, d), lambda i: (i, 0)),
            pl.BlockSpec((tm, d), lambda i: (i, 0)),
            pl.BlockSpec((tm, d), lambda i: (i + nt, 0)),
            pl.BlockSpec((tm, LANES), lambda i: (i, 0)),
        ],
        out_specs=pl.BlockSpec((tm, d), lambda i: (i, 0)),
        out_shape=jax.ShapeDtypeStruct((n, d), F32),
        compiler_params=_cparams("parallel"),
        name="moe_combine",
    )(h, y2, y2, info)


def _hier_moe_residual(h, ffn_g, w_rg, b_rg, w_re, b_re, w_gate, w_up, w_down):
    xn, info = _router(h, ffn_g, w_rg, b_rg, w_re, b_re)
    plan = _dispatch_plan(info, h.shape[0])
    y2 = _moe_experts(xn, plan, w_gate, w_up, w_down)
    return _combine(h, y2, info)


def _tile_gain(g, n_heads, scale=1.0):
    return jnp.tile(g.astype(F32) * scale, n_heads)


def _mem_kv(mem2d, g_mem, w_mkv, gk):
    width = N_MEM_HEADS * HEAD_DIM
    gains = jnp.concatenate([_tile_gain(gk, N_MEM_HEADS), jnp.ones((width,), F32)])
    modes = jnp.array([1, 0], jnp.int32)
    return _norm_proj(mem2d, g_mem, w_mkv.astype(BF16), gains, modes, tn=width)


def kernel(x, mem, rel_bias, attn_norm_g, ffn_norm_g, mem_norm_g, w_mem_kv, q_norm_mem, k_norm_mem, w_out, w_in_a, q_norm_a, k_norm_a, kv_norm_g, w_kv_b, k_norm_b, w_in_b, q_norm_b, lambda_b, subln_g, w_router_group, b_router_group, w_router_expert, b_router_expert, w_gate, w_up, w_down):
    bsz, seq, dm = x.shape
    n_tok = bsz * seq
    n_mem = mem.shape[1]
    d = HEAD_DIM
    mem_width = N_MEM_HEADS * d
    mix_width = dm - mem_width
    moba_heads = mix_width // d
    diff_heads = mix_width // (2 * d)
    qk_w = diff_heads * d
    tn = 512

    h = x.reshape(n_tok, dm)
    mem2d = mem.reshape(bsz * n_mem, dm)
    bias_tiles = _near_bias_tiles(rel_bias)

    gains = jnp.concatenate([
        _tile_gain(q_norm_a[0], moba_heads, ATTN_SCALE),
        _tile_gain(k_norm_a[0], moba_heads),
        jnp.ones((mix_width,), F32),
        _tile_gain(q_norm_mem[0], N_MEM_HEADS, ATTN_SCALE)])
    modes = jnp.array([1] * (2 * mix_width // tn) + [0] * (mix_width // tn) + [1] * (mem_width // tn),
                      jnp.int32)
    proj = _norm_proj(h, attn_norm_g[0], w_in_a[0].astype(BF16), gains, modes, tn=tn)
    proj = proj.reshape(bsz, seq, -1)
    kvm = _mem_kv(mem2d, mem_norm_g[0], w_mem_kv[0], k_norm_mem[0]).reshape(bsz, n_mem, -1)
    o_mix = _moba_attention(proj, bias_tiles, n_heads=moba_heads, q_off=0, k_off=moba_heads,
                            v_off=2 * moba_heads)
    o_mem = _mem_attention(proj, kvm, q_off=3 * moba_heads)
    h = _out_proj(o_mix.reshape(n_tok, -1), o_mem.reshape(n_tok, -1), w_out[0].astype(BF16), h)
    h = _hier_moe_residual(h, ffn_norm_g[0], w_router_group[0], b_router_group[0],
                           w_router_expert[0], b_router_expert[0], w_gate[0], w_up[0], w_down[0])

    gains = jnp.concatenate([
        _tile_gain(k_norm_b[0], diff_heads), _tile_gain(k_norm_b[1], diff_heads),
        jnp.ones((diff_heads * DIFF_VDIM,), F32)])
    modes = jnp.array([1] * (2 * qk_w // tn) + [0] * (diff_heads * DIFF_VDIM // tn), jnp.int32)
    kv = _norm_proj(h, kv_norm_g, w_kv_b.astype(BF16), gains, modes, tn=tn).reshape(bsz, seq, -1)

    layer = 1
    lam_init = 0.8 - 0.6 * math.exp(-0.3 * layer)
    lam_vec = lambda_b[0].astype(F32)
    lam = (jnp.exp(jnp.sum(lam_vec[0] * lam_vec[1])) - jnp.exp(jnp.sum(lam_vec[2] * lam_vec[3]))
           + lam_init)
    gains = jnp.concatenate([
        _tile_gain(q_norm_b[0, 0], diff_heads, ATTN_SCALE),
        _tile_gain(q_norm_b[0, 1], diff_heads, ATTN_SCALE),
        _tile_gain(q_norm_mem[1], N_MEM_HEADS, ATTN_SCALE)])
    modes = jnp.ones((dm // tn,), jnp.int32)
    proj = _norm_proj(h, attn_norm_g[1], w_in_b[0].astype(BF16), gains, modes, tn=tn)
    proj = proj.reshape(bsz, seq, -1)
    kvm = _mem_kv(mem2d, mem_norm_g[1], w_mem_kv[1], k_norm_mem[1]).reshape(bsz, n_mem, -1)
    diff_bias = bias_tiles.reshape(diff_heads, 2, 2, MOBA_BLOCK, MOBA_BLOCK)
    o_mix = _diff_attention(proj, kv, diff_bias, lam, subln_g[0], n_heads=diff_heads,
                            out_scale=1.0 - lam_init)
    o_mem = _mem_attention(proj, kvm, q_off=2 * diff_heads)
    h = _out_proj(o_mix.reshape(n_tok, -1), o_mem.reshape(n_tok, -1), w_out[1].astype(BF16), h)
    h = _hier_moe_residual(h, ffn_norm_g[1], w_router_group[1], b_router_group[1],
                           w_router_expert[1], b_router_expert[1], w_gate[1], w_up[1], w_down[1])
    return h.reshape(bsz, seq, dm)
```

```python
import functools
import math

import jax
import jax.numpy as jnp
from jax import lax
from jax.experimental import pallas as pl
from jax.experimental.pallas import tpu as pltpu

F32 = jnp.float32
BF16 = jnp.bfloat16

HEAD_DIM = 128
LANES = 128
N_MEM_HEADS = 4
MOBA_BLOCK = 256
MOBA_TOPK = 3
DIFF_VDIM = 2 * HEAD_DIM
REL_BUCKETS = 32
REL_MAX_DIST = 128
N_GROUPS = 8
EXPERTS_PER_GROUP = 8
N_EXPERTS = N_GROUPS * EXPERTS_PER_GROUP
MOE_BLOCK = 256
RMS_EPS = 1e-6
ATTN_SCALE = HEAD_DIM ** -0.5
MASKED = -1e30
VMEM_LIMIT = 56 * 1024 * 1024


def _cparams(*sem):
    return pltpu.CompilerParams(dimension_semantics=sem, vmem_limit_bytes=VMEM_LIMIT)


def _dot_t(a, b):
    return lax.dot_general(a, b, (((1,), (1,)), ((), ())), preferred_element_type=F32)


def _dot(a, b):
    return jnp.dot(a, b, preferred_element_type=F32)


def _norm_proj_kernel(mode_ref, h_ref, g_ref, w_ref, gain_ref, o_ref, xn_ref):
    j = pl.program_id(1)

    @pl.when(j == 0)
    def _():
        x = h_ref[...]
        ms = jnp.mean(x * x, axis=-1, keepdims=True)
        xn_ref[...] = (x * lax.rsqrt(ms + RMS_EPS) * g_ref[...]).astype(BF16)

    acc = _dot(xn_ref[...], w_ref[...])
    n_heads = acc.shape[1] // HEAD_DIM

    @pl.when(mode_ref[j] == 1)
    def _():
        for hh in range(n_heads):
            sl = slice(hh * HEAD_DIM, (hh + 1) * HEAD_DIM)
            a = acc[:, sl]
            ms = jnp.mean(a * a, axis=-1, keepdims=True)
            o_ref[:, sl] = (a * lax.rsqrt(ms + RMS_EPS) * gain_ref[:, sl]).astype(o_ref.dtype)

    @pl.when(mode_ref[j] != 1)
    def _():
        o_ref[...] = acc.astype(o_ref.dtype)


def _norm_proj(h, g, w, gains, modes, *, tn=512):
    n, d = h.shape
    n_out = w.shape[1]
    tm = min(512, n)
    grid = (n // tm, n_out // tn)
    return pl.pallas_call(
        _norm_proj_kernel,
        grid_spec=pltpu.PrefetchScalarGridSpec(
            num_scalar_prefetch=1,
            grid=grid,
            in_specs=[
                pl.BlockSpec((tm, d), lambda i, j, m: (i, 0)),
                pl.BlockSpec((1, d), lambda i, j, m: (0, 0)),
                pl.BlockSpec((d, tn), lambda i, j, m: (0, j)),
                pl.BlockSpec((1, tn), lambda i, j, m: (0, j)),
            ],
            out_specs=pl.BlockSpec((tm, tn), lambda i, j, m: (i, j)),
            scratch_shapes=[pltpu.VMEM((tm, d), BF16)],
        ),
        out_shape=jax.ShapeDtypeStruct((n, n_out), BF16),
        compiler_params=_cparams("parallel", "arbitrary"),
        name="norm_proj",
    )(modes, h, g.reshape(1, d).astype(F32), w, gains.reshape(1, n_out).astype(F32))


def _t5_bucket(dist):
    n = jnp.maximum(dist, 0)
    max_exact = REL_BUCKETS // 2
    large = max_exact + (jnp.log(jnp.maximum(n, 1).astype(F32) / max_exact)
                         / math.log(REL_MAX_DIST / max_exact)
                         * (REL_BUCKETS - max_exact)).astype(jnp.int32)
    large = jnp.minimum(large, REL_BUCKETS - 1)
    return jnp.where(n < max_exact, n, large)


def _near_bias_tiles(rel_bias):
    t = MOBA_BLOCK
    tab = rel_bias.T.astype(F32)
    by_dist = tab[:, _t5_bucket(jnp.arange(2 * t))]
    far = tab[:, REL_BUCKETS - 1][:, None, None]
    r = jnp.arange(t)[:, None]
    c = jnp.arange(t)[None, :]
    d0 = r - c
    own = jnp.where(d0 >= 0, by_dist[:, jnp.maximum(d0, 0)] - far, MASKED)
    prev = by_dist[:, d0 + t] - far
    return jnp.stack([own, prev], axis=1)


def _online_update(carry, s, v):
    m, l, acc = carry
    m_new = jnp.maximum(m, jnp.max(s, axis=-1, keepdims=True))
    alpha = jnp.exp(m - m_new)
    p = jnp.exp(s - m_new)
    l = alpha * l + jnp.sum(p, axis=-1, keepdims=True)
    acc = alpha * acc + _dot(p.astype(BF16), v)
    return m_new, l, acc


def _moba_kernel(q_ref, k_ref, v_ref, bias_ref, o_ref, kmean_ref, *, nb):
    i = pl.program_id(2)
    t = MOBA_BLOCK

    @pl.when(i == 0)
    def _():
        kmean_ref[...] = jnp.zeros_like(kmean_ref)
        for jb in range(nb):
            kb = k_ref[0, jb * t:(jb + 1) * t, :].astype(F32)
            kmean_ref[jb:jb + 1, :] = jnp.mean(kb, axis=0, keepdims=True)

    q = q_ref[0]

    km = kmean_ref[...]
    km_hi = km.astype(BF16)
    km_lo = (km - km_hi.astype(F32)).astype(BF16)
    gate = _dot_t(q, km_hi) + _dot_t(q, km_lo)
    lane = lax.broadcasted_iota(jnp.int32, (t, LANES), 1).astype(F32)
    neg = jnp.float32(-3e38)
    gcur = jnp.where(lane < i.astype(F32), gate, neg)
    sel = jnp.zeros((t, LANES), jnp.bool_)
    for _ in range(MOBA_TOPK):
        mx = jnp.max(gcur, axis=-1, keepdims=True)
        idx = jnp.min(jnp.where(gcur == mx, lane, float(LANES)), axis=-1, keepdims=True)
        hit = lane == idx
        sel = sel | (hit & (mx > 0.5 * neg))
        gcur = jnp.where(hit, neg, gcur)
    pen = jnp.where(sel, 0.0, MASKED).astype(BF16)
    q_aug = jnp.concatenate([q, pen], axis=1)

    def k_aug(j):
        start = pl.multiple_of(j * t, t)
        kb = k_ref[0, pl.ds(start, t), :]
        onehot = (lax.broadcasted_iota(jnp.int32, (t, LANES), 1) == j).astype(BF16)
        return jnp.concatenate([kb, onehot], axis=1)

    def v_blk(j):
        start = pl.multiple_of(j * t, t)
        return v_ref[0, pl.ds(start, t), :]

    start_i = pl.multiple_of(i * t, t)
    s = _dot_t(q, k_ref[0, pl.ds(start_i, t), :]) + bias_ref[0, 0]
    m = jnp.max(s, axis=-1, keepdims=True)
    p = jnp.exp(s - m)
    carry = (m, jnp.sum(p, axis=-1, keepdims=True), _dot(p.astype(BF16), v_blk(i)))

    def prev_block(c):
        s = _dot_t(q_aug, k_aug(i - 1)) + bias_ref[0, 1]
        return _online_update(c, s, v_blk(i - 1))

    carry = lax.cond(i >= 1, prev_block, lambda c: c, carry)

    def far_block(j, c):
        return _online_update(c, _dot_t(q_aug, k_aug(j)), v_blk(j))

    m, l, acc = lax.fori_loop(0, jnp.maximum(i - 1, 0), far_block, carry)
    o_ref[0] = (acc / l).astype(o_ref.dtype)


def _moba_attention(proj, bias_tiles, *, n_heads, q_off, k_off, v_off):
    b, s, _ = proj.shape
    t = MOBA_BLOCK
    nb = s // t
    d = HEAD_DIM
    return pl.pallas_call(
        functools.partial(_moba_kernel, nb=nb),
        grid=(b, n_heads, nb),
        in_specs=[
            pl.BlockSpec((1, t, d), lambda bb, h, i: (bb, i, q_off + h)),
            pl.BlockSpec((1, s, d), lambda bb, h, i: (bb, 0, k_off + h)),
            pl.BlockSpec((1, s, d), lambda bb, h, i: (bb, 0, v_off + h)),
            pl.BlockSpec((1, 2, t, t), lambda bb, h, i: (h, 0, 0, 0)),
        ],
        out_specs=pl.BlockSpec((1, t, d), lambda bb, h, i: (bb, i, h)),
        out_shape=jax.ShapeDtypeStruct((b, s, n_heads * d), BF16),
        scratch_shapes=[pltpu.VMEM((LANES, d), F32)],
        compiler_params=_cparams("parallel", "parallel", "arbitrary"),
        name="moba_attention",
    )(proj, proj, proj, bias_tiles)


def _diff_kernel(lam_ref, q1_ref, q2_ref, k1_ref, k2_ref, v_ref, bias_ref, g_ref, o_ref,
                 *, out_scale):
    i = pl.program_id(2)
    t = MOBA_BLOCK
    qs = (q1_ref[0], q2_ref[0])
    ks = (k1_ref, k2_ref)

    def blk(ref, j):
        start = pl.multiple_of(j * t, t)
        return ref[0, pl.ds(start, t), :]

    carries = []
    for mi in range(2):
        s = _dot_t(qs[mi], blk(ks[mi], i)) + bias_ref[0, mi, 0]
        m = jnp.max(s, axis=-1, keepdims=True)
        p = jnp.exp(s - m)
        carries.append((m, jnp.sum(p, axis=-1, keepdims=True), _dot(p.astype(BF16), blk(v_ref, i))))
    carry = tuple(carries)

    def prev_block(c):
        v = blk(v_ref, i - 1)
        return tuple(
            _online_update(c[mi], _dot_t(qs[mi], blk(ks[mi], i - 1)) + bias_ref[0, mi, 1], v)
            for mi in range(2))

    carry = lax.cond(i >= 1, prev_block, lambda c: c, carry)

    def far_block(j, c):
        v = blk(v_ref, j)
        return tuple(
            _online_update(c[mi], _dot_t(qs[mi], blk(ks[mi], j)), v) for mi in range(2))

    (_, l1, acc1), (_, l2, acc2) = lax.fori_loop(0, jnp.maximum(i - 1, 0), far_block, carry)
    o = acc1 / l1 - lam_ref[0] * (acc2 / l2)
    ms = jnp.mean(o * o, axis=-1, keepdims=True)
    o_ref[0] = (o * lax.rsqrt(ms + RMS_EPS) * g_ref[...] * out_scale).astype(o_ref.dtype)


def _diff_attention(proj, kv, bias_tiles, lam, subln_g, *, n_heads, out_scale):
    b, s, _ = proj.shape
    t = MOBA_BLOCK
    d = HEAD_DIM
    dv = DIFF_VDIM
    return pl.pallas_call(
        functools.partial(_diff_kernel, out_scale=out_scale),
        grid=(b, n_heads, s // t),
        in_specs=[
            pl.BlockSpec(memory_space=pltpu.SMEM),
            pl.BlockSpec((1, t, d), lambda bb, h, i: (bb, i, h)),
            pl.BlockSpec((1, t, d), lambda bb, h, i: (bb, i, n_heads + h)),
            pl.BlockSpec((1, s, d), lambda bb, h, i: (bb, 0, h)),
            pl.BlockSpec((1, s, d), lambda bb, h, i: (bb, 0, n_heads + h)),
            pl.BlockSpec((1, s, dv), lambda bb, h, i: (bb, 0, n_heads + h)),
            pl.BlockSpec((1, 2, 2, t, t), lambda bb, h, i: (h, 0, 0, 0, 0)),
            pl.BlockSpec((1, dv), lambda bb, h, i: (0, 0)),
        ],
        out_specs=pl.BlockSpec((1, t, dv), lambda bb, h, i: (bb, i, h)),
        out_shape=jax.ShapeDtypeStruct((b, s, n_heads * dv), BF16),
        compiler_params=_cparams("parallel", "parallel", "arbitrary"),
        name="diff_attention",
    )(lam.reshape(1).astype(F32), proj, proj, kv, kv, kv, bias_tiles,
      subln_g.reshape(1, dv).astype(F32))


def _mem_attn_kernel(q_ref, k_ref, v_ref, o_ref):
    s = _dot_t(q_ref[0], k_ref[0])
    m = jnp.max(s, axis=-1, keepdims=True)
    p = jnp.exp(s - m)
    l = jnp.sum(p, axis=-1, keepdims=True)
    o_ref[0] = (_dot(p.astype(BF16), v_ref[0]) / l).astype(o_ref.dtype)


def _mem_attention(proj, kvm, *, q_off):
    b, s, _ = proj.shape
    n_mem = kvm.shape[1]
    d = HEAD_DIM
    tq = min(1024, s)
    return pl.pallas_call(
        _mem_attn_kernel,
        grid=(b, N_MEM_HEADS, s // tq),
        in_specs=[
            pl.BlockSpec((1, tq, d), lambda bb, h, i: (bb, i, q_off + h)),
            pl.BlockSpec((1, n_mem, d), lambda bb, h, i: (bb, 0, h)),
            pl.BlockSpec((1, n_mem, d), lambda bb, h, i: (bb, 0, N_MEM_HEADS + h)),
        ],
        out_specs=pl.BlockSpec((1, tq, d), lambda bb, h, i: (bb, i, h)),
        out_shape=jax.ShapeDtypeStruct((b, s, N_MEM_HEADS * d), BF16),
        compiler_params=_cparams("parallel", "parallel", "arbitrary"),
        name="mem_attention",
    )(proj, kvm, kvm)


def _out_proj_kernel(a_ref, b_ref, wa_ref, wb_ref, h_ref, o_ref):
    o_ref[...] = h_ref[...] + _dot(a_ref[...], wa_ref[...]) + _dot(b_ref[...], wb_ref[...])


def _out_proj(o_mix, o_mem, w_out, h, *, tm=512, tn=512):
    n, d = h.shape
    wa = o_mix.shape[1]
    wb = o_mem.shape[1]
    return pl.pallas_call(
        _out_proj_kernel,
        grid=(n // tm, d // tn),
        in_specs=[
            pl.BlockSpec((tm, wa), lambda i, j: (i, 0)),
            pl.BlockSpec((tm, wb), lambda i, j: (i, 0)),
            pl.BlockSpec((wa, tn), lambda i, j: (0, j)),
            pl.BlockSpec((wb, tn), lambda i, j: (0, j)),
            pl.BlockSpec((tm, tn), lambda i, j: (i, j)),
        ],
        out_specs=pl.BlockSpec((tm, tn), lambda i, j: (i, j)),
        out_shape=jax.ShapeDtypeStruct((n, d), F32),
        compiler_params=_cparams("parallel", "arbitrary"),
        name="out_proj",
    )(o_mix, o_mem, w_out[:wa], w_out[wa:], h)


def _router_kernel(h_ref, g_ref, wh_ref, wl_ref, b_ref, xn_ref, info_ref):
    x = h_ref[...]
    ms = jnp.mean(x * x, axis=-1, keepdims=True)
    xn = x * lax.rsqrt(ms + RMS_EPS) * g_ref[...]
    xn_ref[...] = xn
    xh = xn.astype(BF16)
    xl = (xn - xh.astype(F32)).astype(BF16)
    logits = _dot(xh, wh_ref[...]) + _dot(xl, wh_ref[...]) + _dot(xh, wl_ref[...]) + b_ref[...]

    tm = logits.shape[0]
    lane_i = lax.broadcasted_iota(jnp.int32, (tm, LANES), 1)
    lane = lane_i.astype(F32)
    neg = jnp.float32(-3e38)
    big = float(LANES)

    def top1(vals):
        mx = jnp.max(vals, axis=-1, keepdims=True)
        idx = jnp.min(jnp.where(vals == mx, lane, big), axis=-1, keepdims=True)
        return mx, idx

    is_group = lane_i < N_GROUPS
    gl = jnp.where(is_group, logits, neg)
    gmax, gidx = top1(gl)
    p_g = 1.0 / jnp.sum(jnp.where(is_group, jnp.exp(logits - gmax), 0.0), axis=-1, keepdims=True)

    e_lo = N_GROUPS + gidx * EXPERTS_PER_GROUP
    member = (lane >= e_lo) & (lane < e_lo + EXPERTS_PER_GROUP)
    el = jnp.where(member, logits, neg)
    v1, i1 = top1(el)
    v2, i2 = top1(jnp.where(lane == i1, neg, el))
    w2 = jnp.exp(v2 - v1)
    w1 = 1.0 / (1.0 + w2)
    info = jnp.where(lane_i == 0, i1 - N_GROUPS, 0.0)
    info = jnp.where(lane_i == 1, i2 - N_GROUPS, info)
    info = jnp.where(lane_i == 2, p_g * w1, info)
    info = jnp.where(lane_i == 3, p_g * (w2 * w1), info)
    info_ref[...] = info


def _router(h, g, w_rg, b_rg, w_re, b_re, *, tm=512):
    n, d = h.shape
    pad = LANES - N_GROUPS - N_EXPERTS
    w = jnp.concatenate([w_rg, w_re, jnp.zeros((d, pad), F32)], axis=1).astype(F32)
    bias = jnp.concatenate([b_rg, b_re, jnp.zeros((pad,), F32)]).astype(F32).reshape(1, LANES)
    w_hi = w.astype(BF16)
    w_lo = (w - w_hi.astype(F32)).astype(BF16)
    return pl.pallas_call(
        _router_kernel,
        grid=(n // tm,),
        in_specs=[
            pl.BlockSpec((tm, d), lambda i: (i, 0)),
            pl.BlockSpec((1, d), lambda i: (0, 0)),
            pl.BlockSpec((d, LANES), lambda i: (0, 0)),
            pl.BlockSpec((d, LANES), lambda i: (0, 0)),
            pl.BlockSpec((1, LANES), lambda i: (0, 0)),
        ],
        out_specs=[
            pl.BlockSpec((tm, d), lambda i: (i, 0)),
            pl.BlockSpec((tm, LANES), lambda i: (i, 0)),
        ],
        out_shape=[jax.ShapeDtypeStruct((n, d), F32), jax.ShapeDtypeStruct((n, LANES), F32)],
        compiler_params=_cparams("parallel"),
        name="moe_router",
    )(h, g.reshape(1, d).astype(F32), w_hi, w_lo, bias)


def _dispatch_plan(info, n_tok):
    n_asg = 2 * n_tok
    e_flat = info[:, :2].astype(jnp.int32).reshape(-1)
    order = jnp.argsort(e_flat).astype(jnp.int32)
    e_s = e_flat[order]
    counts = jnp.zeros((N_EXPERTS,), jnp.int32).at[e_flat].add(1)
    starts = jnp.cumsum(counts) - counts
    padded = (counts + MOE_BLOCK - 1) // MOE_BLOCK * MOE_BLOCK
    pends = jnp.cumsum(padded)
    pstarts = pends - padded
    dest = pstarts[e_s] + (jnp.arange(n_asg, dtype=jnp.int32) - starts[e_s])
    n_blocks = -(-n_asg // MOE_BLOCK) + N_EXPERTS
    n_rows = n_blocks * MOE_BLOCK
    tok = order // 2
    slot = order - 2 * tok
    buf_tok = jnp.zeros((n_rows,), jnp.int32).at[dest].set(tok)
    row = jnp.arange(-MOE_BLOCK, n_rows, dtype=jnp.int32)
    dump = n_asg + ((row // MOE_BLOCK) % 2) * MOE_BLOCK + row % MOE_BLOCK
    buf_dst = dump.at[dest + MOE_BLOCK].set(slot * n_tok + tok)
    blk_e = jnp.minimum(
        jnp.searchsorted(pends, jnp.arange(n_blocks, dtype=jnp.int32) * MOE_BLOCK, side='right'),
        N_EXPERTS - 1).astype(jnp.int32)
    n_used = (pends[-1] // MOE_BLOCK).astype(jnp.int32).reshape(1)
    return (buf_tok.reshape(n_blocks, 1, MOE_BLOCK), buf_dst.reshape(n_blocks + 1, 1, MOE_BLOCK),
            blk_e, n_used)


def _moe_kernel(blk_e_ref, n_used_ref, tok_cur, tok_next, dst_prev, x_hbm, wg_ref, wu_ref, wd_ref,
                y_hbm, xb0, xb1, yb0, yb1, wg_bf, wu_bf, wd_bf, sem_in, sem_out):
    b = pl.program_id(0)
    n_used = n_used_ref[0]
    rows = MOE_BLOCK
    xbufs = (xb0, xb1)
    ybufs = (yb0, yb1)

    def gather_copy(p, r, t):
        return pltpu.make_async_copy(x_hbm.at[pl.ds(t, 1), :], xbufs[p].at[pl.ds(r, 1), :],
                                     sem_in.at[p])

    def scatter_copy(p, r, d):
        return pltpu.make_async_copy(ybufs[p].at[pl.ds(r, 1), :], y_hbm.at[pl.ds(d, 1), :],
                                     sem_out.at[p])

    def start_gather(tok_ref, p):
        for r in range(rows):
            gather_copy(p, r, tok_ref[0, 0, r]).start()

    def wait_gather(p):
        for r in range(rows):
            gather_copy(p, r, 0).wait()

    def start_scatter(p):
        for r in range(rows):
            scatter_copy(p, r, dst_prev[0, 0, r]).start()

    def wait_scatter(p):
        for r in range(rows):
            scatter_copy(p, r, 0).wait()

    @pl.when(b == 0)
    def _():
        start_gather(tok_cur, 0)
        yb1[...] = jnp.zeros_like(yb1)
        init = pltpu.make_async_copy(yb1, y_hbm.at[pl.ds(y_hbm.shape[0] - 2 * rows, rows), :],
                                     sem_out.at[0])
        init.start()
        init.wait()

    for p in (0, 1):
        q = 1 - p

        @pl.when((b % 2 == p) & (b <= n_used))
        def _():
            @pl.when(b >= 1)
            def _():
                wait_scatter(p)

            wait_gather(p)

            @pl.when(b < n_used)
            def _():
                e = blk_e_ref[b]
                e_prev = blk_e_ref[jnp.maximum(b - 1, 0)]

                @pl.when((b == 0) | (e != e_prev))
                def _():
                    wg_bf[...] = wg_ref[0, 0].astype(BF16)
                    wu_bf[...] = wu_ref[0, 0].astype(BF16)
                    wd_bf[...] = wd_ref[0, 0].astype(BF16)

                start_gather(tok_next, q)
                start_scatter(q)
                x = xbufs[p][...].astype(BF16)
                hg = _dot(x, wg_bf[...])
                hu = _dot(x, wu_bf[...])
                hdn = (hg * jax.nn.sigmoid(hg)) * hu
                ybufs[p][...] = _dot(hdn.astype(BF16), wd_bf[...])

            @pl.when(b == n_used)
            def _():
                start_scatter(q)
                wait_scatter(q)


def _moe_experts(xn, plan, w_gate, w_up, w_down, layer):
    buf_tok, buf_dst, blk_e, n_used = plan
    n_tok, d = xn.shape
    ff = w_gate.shape[3]
    n_blocks = buf_tok.shape[0]
    last = n_blocks - 1

    def smem_blk(index):
        return pl.BlockSpec((1, 1, MOE_BLOCK), lambda b, be, nu: (index(b), 0, 0),
                            memory_space=pltpu.SMEM)

    def w_spec(shape):
        return pl.BlockSpec((1, 1) + shape,
                            lambda b, be, nu: (layer, be[jnp.minimum(b, last)], 0, 0))

    return pl.pallas_call(
        _moe_kernel,
        grid_spec=pltpu.PrefetchScalarGridSpec(
            num_scalar_prefetch=2,
            grid=(n_blocks + 1,),
            in_specs=[
                smem_blk(lambda b: jnp.minimum(b, last)),
                smem_blk(lambda b: jnp.minimum(b + 1, last)),
                smem_blk(lambda b: b),
                pl.BlockSpec(memory_space=pl.ANY),
                w_spec((d, ff)),
                w_spec((d, ff)),
                w_spec((ff, d)),
            ],
            out_specs=pl.BlockSpec(memory_space=pl.ANY),
            scratch_shapes=[
                pltpu.VMEM((MOE_BLOCK, d), F32),
                pltpu.VMEM((MOE_BLOCK, d), F32),
                pltpu.VMEM((MOE_BLOCK, d), F32),
                pltpu.VMEM((MOE_BLOCK, d), F32),
                pltpu.VMEM((d, ff), BF16),
                pltpu.VMEM((d, ff), BF16),
                pltpu.VMEM((ff, d), BF16),
                pltpu.SemaphoreType.DMA((2,)),
                pltpu.SemaphoreType.DMA((2,)),
            ],
        ),
        out_shape=jax.ShapeDtypeStruct((2 * n_tok + 2 * MOE_BLOCK, d), F32),
        compiler_params=_cparams("arbitrary"),
        name="moe_experts",
    )(blk_e, n_used, buf_tok, buf_tok, buf_dst, xn, w_gate, w_up, w_down)


def _combine_kernel(h_ref, y0_ref, y1_ref, info_ref, o_ref):
    info = info_ref[...]
    o_ref[...] = h_ref[...] + (info[:, 2:3] * y0_ref[...] + info[:, 3:4] * y1_ref[...])


def _combine(h, y2, info, *, tm=512):
    n, d = h.shape
    nt = n // tm
    return pl.pallas_call(
        _combine_kernel,
        grid=(nt,),
        in_specs=[
            pl.BlockSpec((tm, d), lambda i: (i, 0)),
            pl.BlockSpec((tm, d), lambda i: (i, 0)),
            pl.BlockSpec((tm, d), lambda i: (i + nt, 0)),
            pl.BlockSpec((tm, LANES), lambda i: (i, 0)),
        ],
        out_specs=pl.BlockSpec((tm, d), lambda i: (i, 0)),
        out_shape=jax.ShapeDtypeStruct((n, d), F32),
        compiler_params=_cparams("parallel"),
        name="moe_combine",
    )(h, y2, y2, info)


def _hier_moe_residual(h, ffn_g, w_rg, b_rg, w_re, b_re, w_gate, w_up, w_down, layer):
    xn, info = _router(h, ffn_g, w_rg, b_rg, w_re, b_re)
    plan = _dispatch_plan(info, h.shape[0])
    y2 = _moe_experts(xn, plan, w_gate, w_up, w_down, layer)
    return _combine(h, y2, info)


def _tile_gain(g, n_heads, scale=1.0):
    return jnp.tile(g.astype(F32) * scale, n_heads)


def _mem_kv(mem2d, g_mem, w_mkv, gk):
    width = N_MEM_HEADS * HEAD_DIM
    gains = jnp.concatenate([_tile_gain(gk, N_MEM_HEADS), jnp.ones((width,), F32)])
    modes = jnp.array([1, 0], jnp.int32)
    return _norm_proj(mem2d, g_mem, w_mkv.astype(BF16), gains, modes, tn=width)


def kernel(x, mem, rel_bias, attn_norm_g, ffn_norm_g, mem_norm_g, w_mem_kv, q_norm_mem, k_norm_mem, w_out, w_in_a, q_norm_a, k_norm_a, kv_norm_g, w_kv_b, k_norm_b, w_in_b, q_norm_b, lambda_b, subln_g, w_router_group, b_router_group, w_router_expert, b_router_expert, w_gate, w_up, w_down):
    bsz, seq, dm = x.shape
    n_tok = bsz * seq
    n_mem = mem.shape[1]
    d = HEAD_DIM
    mem_width = N_MEM_HEADS * d
    mix_width = dm - mem_width
    moba_heads = mix_width // d
    diff_heads = mix_width // (2 * d)
    qk_w = diff_heads * d
    tn = 512

    h = x.reshape(n_tok, dm)
    mem2d = mem.reshape(bsz * n_mem, dm)
    bias_tiles = _near_bias_tiles(rel_bias)

    gains = jnp.concatenate([
        _tile_gain(q_norm_a[0], moba_heads, ATTN_SCALE),
        _tile_gain(k_norm_a[0], moba_heads),
        jnp.ones((mix_width,), F32),
        _tile_gain(q_norm_mem[0], N_MEM_HEADS, ATTN_SCALE)])
    modes = jnp.array([1] * (2 * mix_width // tn) + [0] * (mix_width // tn) + [1] * (mem_width // tn),
                      jnp.int32)
    proj = _norm_proj(h, attn_norm_g[0], w_in_a[0].astype(BF16), gains, modes, tn=tn)
    proj = proj.reshape(bsz, seq, -1)
    kvm = _mem_kv(mem2d, mem_norm_g[0], w_mem_kv[0], k_norm_mem[0]).reshape(bsz, n_mem, -1)
    o_mix = _moba_attention(proj, bias_tiles, n_heads=moba_heads, q_off=0, k_off=moba_heads,
                            v_off=2 * moba_heads)
    o_mem = _mem_attention(proj, kvm, q_off=3 * moba_heads)
    h = _out_proj(o_mix.reshape(n_tok, -1), o_mem.reshape(n_tok, -1), w_out[0].astype(BF16), h)
    h = _hier_moe_residual(h, ffn_norm_g[0], w_router_group[0], b_router_group[0],
                           w_router_expert[0], b_router_expert[0], w_gate, w_up, w_down, 0)

    gains = jnp.concatenate([
        _tile_gain(k_norm_b[0], diff_heads), _tile_gain(k_norm_b[1], diff_heads),
        jnp.ones((diff_heads * DIFF_VDIM,), F32)])
    modes = jnp.array([1] * (2 * qk_w // tn) + [0] * (diff_heads * DIFF_VDIM // tn), jnp.int32)
    kv = _norm_proj(h, kv_norm_g, w_kv_b.astype(BF16), gains, modes, tn=tn).reshape(bsz, seq, -1)

    layer = 1
    lam_init = 0.8 - 0.6 * math.exp(-0.3 * layer)
    lam_vec = lambda_b[0].astype(F32)
    lam = (jnp.exp(jnp.sum(lam_vec[0] * lam_vec[1])) - jnp.exp(jnp.sum(lam_vec[2] * lam_vec[3]))
           + lam_init)
    gains = jnp.concatenate([
        _tile_gain(q_norm_b[0, 0], diff_heads, ATTN_SCALE),
        _tile_gain(q_norm_b[0, 1], diff_heads, ATTN_SCALE),
        _tile_gain(q_norm_mem[1], N_MEM_HEADS, ATTN_SCALE)])
    modes = jnp.ones((dm // tn,), jnp.int32)
    proj = _norm_proj(h, attn_norm_g[1], w_in_b[0].astype(BF16), gains, modes, tn=tn)
    proj = proj.reshape(bsz, seq, -1)
    kvm = _mem_kv(mem2d, mem_norm_g[1], w_mem_kv[1], k_norm_mem[1]).reshape(bsz, n_mem, -1)
    diff_bias = bias_tiles.reshape(diff_heads, 2, 2, MOBA_BLOCK, MOBA_BLOCK)
    o_mix = _diff_attention(proj, kv, diff_bias, lam, subln_g[0], n_heads=diff_heads,
                            out_scale=1.0 - lam_init)
    o_mem = _mem_attention(proj, kvm, q_off=2 * diff_heads)
    h = _out_proj(o_mix.reshape(n_tok, -1), o_mem.reshape(n_tok, -1), w_out[1].astype(BF16), h)
    h = _hier_moe_residual(h, ffn_norm_g[1], w_router_group[1], b_router_group[1],
                           w_router_expert[1], b_router_expert[1], w_gate, w_up, w_down, 1)
    return h.reshape(bsz, seq, dm)
```

```python
import functools
import math

import jax
import jax.numpy as jnp
from jax import lax
from jax.experimental import pallas as pl
from jax.experimental.pallas import tpu as pltpu

F32 = jnp.float32
BF16 = jnp.bfloat16

HEAD_DIM = 128
LANES = 128
N_MEM_HEADS = 4
MOBA_BLOCK = 256
MOBA_TOPK = 3
ATTN_TQ = 4 * MOBA_BLOCK
DIFF_VDIM = 2 * HEAD_DIM
REL_BUCKETS = 32
REL_MAX_DIST = 128
N_GROUPS = 8
EXPERTS_PER_GROUP = 8
N_EXPERTS = N_GROUPS * EXPERTS_PER_GROUP
MOE_BLOCK = 256
RMS_EPS = 1e-6
ATTN_SCALE = HEAD_DIM ** -0.5
MASKED = -1e30
VMEM_LIMIT = 56 * 1024 * 1024


def _cparams(*sem):
    return pltpu.CompilerParams(dimension_semantics=sem, vmem_limit_bytes=VMEM_LIMIT)


def _dot_t(a, b):
    return lax.dot_general(a, b, (((1,), (1,)), ((), ())), preferred_element_type=F32)


def _dot(a, b):
    return jnp.dot(a, b, preferred_element_type=F32)


def _norm_proj_kernel(mode_ref, h_ref, g_ref, w_ref, gain_ref, o_ref, xn_ref):
    j = pl.program_id(1)

    @pl.when(j == 0)
    def _():
        x = h_ref[...]
        ms = jnp.mean(x * x, axis=-1, keepdims=True)
        xn_ref[...] = (x * lax.rsqrt(ms + RMS_EPS) * g_ref[...]).astype(BF16)

    acc = _dot(xn_ref[...], w_ref[...])
    n_heads = acc.shape[1] // HEAD_DIM

    @pl.when(mode_ref[j] == 1)
    def _():
        for hh in range(n_heads):
            sl = slice(hh * HEAD_DIM, (hh + 1) * HEAD_DIM)
            a = acc[:, sl]
            ms = jnp.mean(a * a, axis=-1, keepdims=True)
            o_ref[:, sl] = (a * lax.rsqrt(ms + RMS_EPS) * gain_ref[:, sl]).astype(o_ref.dtype)

    @pl.when(mode_ref[j] != 1)
    def _():
        o_ref[...] = acc.astype(o_ref.dtype)


def _norm_proj(h, g, w, gains, modes, *, tn=512):
    n, d = h.shape
    n_out = w.shape[1]
    tm = min(512, n)
    grid = (n // tm, n_out // tn)
    return pl.pallas_call(
        _norm_proj_kernel,
        grid_spec=pltpu.PrefetchScalarGridSpec(
            num_scalar_prefetch=1,
            grid=grid,
            in_specs=[
                pl.BlockSpec((tm, d), lambda i, j, m: (i, 0)),
                pl.BlockSpec((1, d), lambda i, j, m: (0, 0)),
                pl.BlockSpec((d, tn), lambda i, j, m: (0, j)),
                pl.BlockSpec((1, tn), lambda i, j, m: (0, j)),
            ],
            out_specs=pl.BlockSpec((tm, tn), lambda i, j, m: (i, j)),
            scratch_shapes=[pltpu.VMEM((tm, d), BF16)],
        ),
        out_shape=jax.ShapeDtypeStruct((n, n_out), BF16),
        compiler_params=_cparams("parallel", "arbitrary"),
        name="norm_proj",
    )(modes, h, g.reshape(1, d).astype(F32), w, gains.reshape(1, n_out).astype(F32))


def _t5_bucket(dist):
    n = jnp.maximum(dist, 0)
    max_exact = REL_BUCKETS // 2
    large = max_exact + (jnp.log(jnp.maximum(n, 1).astype(F32) / max_exact)
                         / math.log(REL_MAX_DIST / max_exact)
                         * (REL_BUCKETS - max_exact)).astype(jnp.int32)
    large = jnp.minimum(large, REL_BUCKETS - 1)
    return jnp.where(n < max_exact, n, large)


def _near_bias_tiles(rel_bias):
    t = MOBA_BLOCK
    tab = rel_bias.T.astype(F32)
    by_dist = tab[:, _t5_bucket(jnp.arange(2 * t))]
    far = tab[:, REL_BUCKETS - 1][:, None, None]
    r = jnp.arange(t)[:, None]
    c = jnp.arange(t)[None, :]
    d0 = r - c
    own = jnp.where(d0 >= 0, by_dist[:, jnp.maximum(d0, 0)] - far, MASKED)
    prev = by_dist[:, d0 + t] - far
    return jnp.stack([own, prev], axis=1)


def _flash_update(m_ref, l_ref, acc_ref, idx, s, v):
    m_old = m_ref[idx]
    m_new = jnp.maximum(m_old, jnp.max(s, axis=-1, keepdims=True))
    alpha = jnp.exp(m_old - m_new)
    p = jnp.exp(s - m_new)
    l_ref[idx] = alpha * l_ref[idx] + jnp.sum(p, axis=-1, keepdims=True)
    acc_ref[idx] = alpha * acc_ref[idx] + _dot(p.astype(BF16), v)
    m_ref[idx] = m_new


def _flash_init(m_ref, l_ref, acc_ref):
    m_ref[...] = jnp.full(m_ref.shape, -3e38, F32)
    l_ref[...] = jnp.zeros_like(l_ref)
    acc_ref[...] = jnp.zeros_like(acc_ref)


def _add_corner(s, corner):
    w = s.shape[1] - MOBA_BLOCK
    return jnp.concatenate([s[:, :w], s[:, w:] + corner], axis=1)


def _add_near_bias(s, a, prev_tile, own_tile):
    t = MOBA_BLOCK
    pieces = []
    if a >= 2:
        pieces.append(s[:, :(a - 1) * t])
    if a >= 1:
        pieces.append(s[:, (a - 1) * t:a * t] + prev_tile)
    pieces.append(s[:, a * t:] + own_tile)
    return pieces[0] if len(pieces) == 1 else jnp.concatenate(pieces, axis=1)


def _moba_kernel(q_ref, k_ref, v_ref, bias_ref, o_ref,
                 kmean_ref, qaug_ref, kaug_ref, m_ref, l_ref, acc_ref, *, nb):
    ti = pl.program_id(2)
    t = MOBA_BLOCK
    tq = ATTN_TQ
    n_sub = tq // t
    every = slice(None)

    @pl.when(ti == 0)
    def _():
        kmean_ref[...] = jnp.zeros_like(kmean_ref)
        for jb in range(nb):
            kb = k_ref[0, jb * t:(jb + 1) * t, :].astype(F32)
            kmean_ref[jb:jb + 1, :] = jnp.mean(kb, axis=0, keepdims=True)

    km = kmean_ref[...]
    km_hi = km.astype(BF16)
    km_lo = (km - km_hi.astype(F32)).astype(BF16)
    lane_i = lax.broadcasted_iota(jnp.int32, (t, LANES), 1)
    lane = lane_i.astype(F32)
    neg = jnp.float32(-3e38)
    for a in range(n_sub):
        rows = slice(a * t, (a + 1) * t)
        q = q_ref[0, rows, :]
        own = ti * n_sub + a
        gate = _dot_t(q, km_hi) + _dot_t(q, km_lo)
        gcur = jnp.where(lane_i < own, gate, neg)
        sel = lane_i == own
        for _ in range(MOBA_TOPK):
            mx = jnp.max(gcur, axis=-1, keepdims=True)
            idx = jnp.min(jnp.where(gcur == mx, lane, float(LANES)), axis=-1, keepdims=True)
            hit = lane == idx
            sel = sel | (hit & (mx > 0.5 * neg))
            gcur = jnp.where(hit, neg, gcur)
        pen = jnp.where(sel, 0.0, MASKED).astype(BF16)
        qaug_ref[rows, :] = jnp.concatenate([q, pen], axis=1)

    _flash_init(m_ref, l_ref, acc_ref)

    def build_kaug(c):
        start = pl.multiple_of(c * tq, tq)
        kb = k_ref[0, pl.ds(start, tq), :]
        blk = c * n_sub + lax.broadcasted_iota(jnp.int32, (tq, LANES), 0) // t
        onehot = (lax.broadcasted_iota(jnp.int32, (tq, LANES), 1) == blk).astype(BF16)
        kaug_ref[...] = jnp.concatenate([kb, onehot], axis=1)
        return start

    def far_chunk(c, carry):
        start = build_kaug(c)
        v = v_ref[0, pl.ds(start, tq), :]
        for a in range(n_sub):
            rows = slice(a * t, (a + 1) * t)
            s = _dot_t(qaug_ref[rows, :], kaug_ref[...])
            if a == 0:
                s = _add_corner(s, jnp.where(c == ti - 1, bias_ref[0, 1], 0.0))
            _flash_update(m_ref, l_ref, acc_ref, (rows, every), s, v)
        return carry

    lax.fori_loop(0, ti, far_chunk, 0)

    start = build_kaug(ti)
    for a in range(n_sub):
        rows = slice(a * t, (a + 1) * t)
        nk = (a + 1) * t
        s = _dot_t(qaug_ref[rows, :], kaug_ref[:nk, :])
        s = _add_near_bias(s, a, bias_ref[0, 1], bias_ref[0, 0])
        _flash_update(m_ref, l_ref, acc_ref, (rows, every), s, v_ref[0, pl.ds(start, nk), :])

    o_ref[0] = (acc_ref[...] / l_ref[...]).astype(o_ref.dtype)


def _moba_attention(proj, bias_tiles, *, n_heads, q_off, k_off, v_off):
    b, s, _ = proj.shape
    t = MOBA_BLOCK
    tq = ATTN_TQ
    nb = s // t
    d = HEAD_DIM
    return pl.pallas_call(
        functools.partial(_moba_kernel, nb=nb),
        grid=(b, n_heads, s // tq),
        in_specs=[
            pl.BlockSpec((1, tq, d), lambda bb, h, i: (bb, i, q_off + h)),
            pl.BlockSpec((1, s, d), lambda bb, h, i: (bb, 0, k_off + h)),
            pl.BlockSpec((1, s, d), lambda bb, h, i: (bb, 0, v_off + h)),
            pl.BlockSpec((1, 2, t, t), lambda bb, h, i: (h, 0, 0, 0)),
        ],
        out_specs=pl.BlockSpec((1, tq, d), lambda bb, h, i: (bb, i, h)),
        out_shape=jax.ShapeDtypeStruct((b, s, n_heads * d), BF16),
        scratch_shapes=[
            pltpu.VMEM((LANES, d), F32),
            pltpu.VMEM((tq, d + LANES), BF16),
            pltpu.VMEM((tq, d + LANES), BF16),
            pltpu.VMEM((tq, 1), F32),
            pltpu.VMEM((tq, 1), F32),
            pltpu.VMEM((tq, d), F32),
        ],
        compiler_params=_cparams("parallel", "parallel", "arbitrary"),
        name="moba_attention",
    )(proj, proj, proj, bias_tiles)


def _diff_kernel(lam_ref, q1_ref, q2_ref, k1_ref, k2_ref, v_ref, bias_ref, g_ref, o_ref,
                 m_ref, l_ref, acc_ref, *, out_scale):
    ti = pl.program_id(2)
    t = MOBA_BLOCK
    tq = ATTN_TQ
    n_sub = tq // t
    every = slice(None)
    q_refs = (q1_ref, q2_ref)
    k_refs = (k1_ref, k2_ref)

    _flash_init(m_ref, l_ref, acc_ref)

    def far_chunk(c, carry):
        start = pl.multiple_of(c * tq, tq)
        v = v_ref[0, pl.ds(start, tq), :]
        for mi in range(2):
            k = k_refs[mi][0, pl.ds(start, tq), :]
            for a in range(n_sub):
                rows = slice(a * t, (a + 1) * t)
                s = _dot_t(q_refs[mi][0, rows, :], k)
                if a == 0:
                    s = _add_corner(s, jnp.where(c == ti - 1, bias_ref[0, mi, 1], 0.0))
                _flash_update(m_ref, l_ref, acc_ref, (mi, rows, every), s, v)
        return carry

    lax.fori_loop(0, ti, far_chunk, 0)

    start = pl.multiple_of(ti * tq, tq)
    for mi in range(2):
        for a in range(n_sub):
            rows = slice(a * t, (a + 1) * t)
            nk = (a + 1) * t
            s = _dot_t(q_refs[mi][0, rows, :], k_refs[mi][0, pl.ds(start, nk), :])
            s = _add_near_bias(s, a, bias_ref[0, mi, 1], bias_ref[0, mi, 0])
            _flash_update(m_ref, l_ref, acc_ref, (mi, rows, every), s,
                          v_ref[0, pl.ds(start, nk), :])

    o = acc_ref[0] / l_ref[0] - lam_ref[0] * (acc_ref[1] / l_ref[1])
    ms = jnp.mean(o * o, axis=-1, keepdims=True)
    o_ref[0] = (o * lax.rsqrt(ms + RMS_EPS) * g_ref[...] * out_scale).astype(o_ref.dtype)


def _diff_attention(proj, kv, bias_tiles, lam, subln_g, *, n_heads, out_scale):
    b, s, _ = proj.shape
    t = MOBA_BLOCK
    tq = ATTN_TQ
    d = HEAD_DIM
    dv = DIFF_VDIM
    return pl.pallas_call(
        functools.partial(_diff_kernel, out_scale=out_scale),
        grid=(b, n_heads, s // tq),
        in_specs=[
            pl.BlockSpec(memory_space=pltpu.SMEM),
            pl.BlockSpec((1, tq, d), lambda bb, h, i: (bb, i, h)),
            pl.BlockSpec((1, tq, d), lambda bb, h, i: (bb, i, n_heads + h)),
            pl.BlockSpec((1, s, d), lambda bb, h, i: (bb, 0, h)),
            pl.BlockSpec((1, s, d), lambda bb, h, i: (bb, 0, n_heads + h)),
            pl.BlockSpec((1, s, dv), lambda bb, h, i: (bb, 0, n_heads + h)),
            pl.BlockSpec((1, 2, 2, t, t), lambda bb, h, i: (h, 0, 0, 0, 0)),
            pl.BlockSpec((1, dv), lambda bb, h, i: (0, 0)),
        ],
        out_specs=pl.BlockSpec((1, tq, dv), lambda bb, h, i: (bb, i, h)),
        out_shape=jax.ShapeDtypeStruct((b, s, n_heads * dv), BF16),
        scratch_shapes=[
            pltpu.VMEM((2, tq, 1), F32),
            pltpu.VMEM((2, tq, 1), F32),
            pltpu.VMEM((2, tq, dv), F32),
        ],
        compiler_params=_cparams("parallel", "parallel", "arbitrary"),
        name="diff_attention",
    )(lam.reshape(1).astype(F32), proj, proj, kv, kv, kv, bias_tiles,
      subln_g.reshape(1, dv).astype(F32))


def _mem_attn_kernel(q_ref, k_ref, v_ref, o_ref):
    s = _dot_t(q_ref[0], k_ref[0])
    m = jnp.max(s, axis=-1, keepdims=True)
    p = jnp.exp(s - m)
    l = jnp.sum(p, axis=-1, keepdims=True)
    o_ref[0] = (_dot(p.astype(BF16), v_ref[0]) / l).astype(o_ref.dtype)


def _mem_attention(proj, kvm, *, q_off):
    b, s, _ = proj.shape
    n_mem = kvm.shape[1]
    d = HEAD_DIM
    tq = min(1024, s)
    return pl.pallas_call(
        _mem_attn_kernel,
        grid=(b, N_MEM_HEADS, s // tq),
        in_specs=[
            pl.BlockSpec((1, tq, d), lambda bb, h, i: (bb, i, q_off + h)),
            pl.BlockSpec((1, n_mem, d), lambda bb, h, i: (bb, 0, h)),
            pl.BlockSpec((1, n_mem, d), lambda bb, h, i: (bb, 0, N_MEM_HEADS + h)),
        ],
        out_specs=pl.BlockSpec((1, tq, d), lambda bb, h, i: (bb, i, h)),
        out_shape=jax.ShapeDtypeStruct((b, s, N_MEM_HEADS * d), BF16),
        compiler_params=_cparams("parallel", "parallel", "arbitrary"),
        name="mem_attention",
    )(proj, kvm, kvm)


def _out_proj_kernel(a_ref, b_ref, wa_ref, wb_ref, h_ref, o_ref):
    o_ref[...] = h_ref[...] + _dot(a_ref[...], wa_ref[...]) + _dot(b_ref[...], wb_ref[...])


def _out_proj(o_mix, o_mem, w_out, h, *, tm=512, tn=512):
    n, d = h.shape
    wa = o_mix.shape[1]
    wb = o_mem.shape[1]
    return pl.pallas_call(
        _out_proj_kernel,
        grid=(n // tm, d // tn),
        in_specs=[
            pl.BlockSpec((tm, wa), lambda i, j: (i, 0)),
            pl.BlockSpec((tm, wb), lambda i, j: (i, 0)),
            pl.BlockSpec((wa, tn), lambda i, j: (0, j)),
            pl.BlockSpec((wb, tn), lambda i, j: (0, j)),
            pl.BlockSpec((tm, tn), lambda i, j: (i, j)),
        ],
        out_specs=pl.BlockSpec((tm, tn), lambda i, j: (i, j)),
        out_shape=jax.ShapeDtypeStruct((n, d), F32),
        compiler_params=_cparams("parallel", "arbitrary"),
        name="out_proj",
    )(o_mix, o_mem, w_out[:wa], w_out[wa:], h)


def _router_kernel(h_ref, g_ref, wh_ref, wl_ref, b_ref, xn_ref, info_ref):
    x = h_ref[...]
    ms = jnp.mean(x * x, axis=-1, keepdims=True)
    xn = x * lax.rsqrt(ms + RMS_EPS) * g_ref[...]
    xn_ref[...] = xn
    xh = xn.astype(BF16)
    xl = (xn - xh.astype(F32)).astype(BF16)
    logits = _dot(xh, wh_ref[...]) + _dot(xl, wh_ref[...]) + _dot(xh, wl_ref[...]) + b_ref[...]

    tm = logits.shape[0]
    lane_i = lax.broadcasted_iota(jnp.int32, (tm, LANES), 1)
    lane = lane_i.astype(F32)
    neg = jnp.float32(-3e38)
    big = float(LANES)

    def top1(vals):
        mx = jnp.max(vals, axis=-1, keepdims=True)
        idx = jnp.min(jnp.where(vals == mx, lane, big), axis=-1, keepdims=True)
        return mx, idx

    is_group = lane_i < N_GROUPS
    gl = jnp.where(is_group, logits, neg)
    gmax, gidx = top1(gl)
    p_g = 1.0 / jnp.sum(jnp.where(is_group, jnp.exp(logits - gmax), 0.0), axis=-1, keepdims=True)

    e_lo = N_GROUPS + gidx * EXPERTS_PER_GROUP
    member = (lane >= e_lo) & (lane < e_lo + EXPERTS_PER_GROUP)
    el = jnp.where(member, logits, neg)
    v1, i1 = top1(el)
    v2, i2 = top1(jnp.where(lane == i1, neg, el))
    w2 = jnp.exp(v2 - v1)
    w1 = 1.0 / (1.0 + w2)
    info = jnp.where(lane_i == 0, i1 - N_GROUPS, 0.0)
    info = jnp.where(lane_i == 1, i2 - N_GROUPS, info)
    info = jnp.where(lane_i == 2, p_g * w1, info)
    info = jnp.where(lane_i == 3, p_g * (w2 * w1), info)
    info_ref[...] = info


def _router(h, g, w_rg, b_rg, w_re, b_re, *, tm=512):
    n, d = h.shape
    pad = LANES - N_GROUPS - N_EXPERTS
    w = jnp.concatenate([w_rg, w_re, jnp.zeros((d, pad), F32)], axis=1).astype(F32)
    bias = jnp.concatenate([b_rg, b_re, jnp.zeros((pad,), F32)]).astype(F32).reshape(1, LANES)
    w_hi = w.astype(BF16)
    w_lo = (w - w_hi.astype(F32)).astype(BF16)
    return pl.pallas_call(
        _router_kernel,
        grid=(n // tm,),
        in_specs=[
            pl.BlockSpec((tm, d), lambda i: (i, 0)),
            pl.BlockSpec((1, d), lambda i: (0, 0)),
            pl.BlockSpec((d, LANES), lambda i: (0, 0)),
            pl.BlockSpec((d, LANES), lambda i: (0, 0)),
            pl.BlockSpec((1, LANES), lambda i: (0, 0)),
        ],
        out_specs=[
            pl.BlockSpec((tm, d), lambda i: (i, 0)),
            pl.BlockSpec((tm, LANES), lambda i: (i, 0)),
        ],
        out_shape=[jax.ShapeDtypeStruct((n, d), F32), jax.ShapeDtypeStruct((n, LANES), F32)],
        compiler_params=_cparams("parallel"),
        name="moe_router",
    )(h, g.reshape(1, d).astype(F32), w_hi, w_lo, bias)


def _dispatch_plan(info, n_tok):
    n_asg = 2 * n_tok
    e_flat = info[:, :2].astype(jnp.int32).reshape(-1)
    order = jnp.argsort(e_flat).astype(jnp.int32)
    e_s = e_flat[order]
    counts = jnp.zeros((N_EXPERTS,), jnp.int32).at[e_flat].add(1)
    starts = jnp.cumsum(counts) - counts
    padded = (counts + MOE_BLOCK - 1) // MOE_BLOCK * MOE_BLOCK
    pends = jnp.cumsum(padded)
    pstarts = pends - padded
    dest = pstarts[e_s] + (jnp.arange(n_asg, dtype=jnp.int32) - starts[e_s])
    n_blocks = -(-n_asg // MOE_BLOCK) + N_EXPERTS
    n_rows = n_blocks * MOE_BLOCK
    tok = order // 2
    slot = order - 2 * tok
    buf_tok = jnp.zeros((n_rows,), jnp.int32).at[dest].set(tok)
    row = jnp.arange(-MOE_BLOCK, n_rows, dtype=jnp.int32)
    dump = n_asg + ((row // MOE_BLOCK) % 2) * MOE_BLOCK + row % MOE_BLOCK
    buf_dst = dump.at[dest + MOE_BLOCK].set(slot * n_tok + tok)
    blk_e = jnp.minimum(
        jnp.searchsorted(pends, jnp.arange(n_blocks, dtype=jnp.int32) * MOE_BLOCK, side='right'),
        N_EXPERTS - 1).astype(jnp.int32)
    n_used = (pends[-1] // MOE_BLOCK).astype(jnp.int32).reshape(1)
    return (buf_tok.reshape(n_blocks, 1, MOE_BLOCK), buf_dst.reshape(n_blocks + 1, 1, MOE_BLOCK),
            blk_e, n_used)


def _moe_kernel(blk_e_ref, n_used_ref, tok_cur, tok_next, dst_prev, x_hbm, wg_ref, wu_ref, wd_ref,
                y_hbm, xb0, xb1, yb0, yb1, wg_bf, wu_bf, wd_bf, sem_in, sem_out):
    b = pl.program_id(0)
    n_used = n_used_ref[0]
    rows = MOE_BLOCK
    xbufs = (xb0, xb1)
    ybufs = (yb0, yb1)

    def gather_copy(p, r, t):
        return pltpu.make_async_copy(x_hbm.at[pl.ds(t, 1), :], xbufs[p].at[pl.ds(r, 1), :],
                                     sem_in.at[p])

    def scatter_copy(p, r, d):
        return pltpu.make_async_copy(ybufs[p].at[pl.ds(r, 1), :], y_hbm.at[pl.ds(d, 1), :],
                                     sem_out.at[p])

    def start_gather(tok_ref, p):
        for r in range(rows):
            gather_copy(p, r, tok_ref[0, 0, r]).start()

    def wait_gather(p):
        for r in range(rows):
            gather_copy(p, r, 0).wait()

    def start_scatter(p):
        for r in range(rows):
            scatter_copy(p, r, dst_prev[0, 0, r]).start()

    def wait_scatter(p):
        for r in range(rows):
            scatter_copy(p, r, 0).wait()

    @pl.when(b == 0)
    def _():
        start_gather(tok_cur, 0)
        yb1[...] = jnp.zeros_like(yb1)
        init = pltpu.make_async_copy(yb1, y_hbm.at[pl.ds(y_hbm.shape[0] - 2 * rows, rows), :],
                                     sem_out.at[0])
        init.start()
        init.wait()

    for p in (0, 1):
        q = 1 - p

        @pl.when((b % 2 == p) & (b <= n_used))
        def _():
            @pl.when(b >= 1)
            def _():
                wait_scatter(p)

            wait_gather(p)

            @pl.when(b < n_used)
            def _():
                e = blk_e_ref[b]
                e_prev = blk_e_ref[jnp.maximum(b - 1, 0)]

                @pl.when((b == 0) | (e != e_prev))
                def _():
                    wg_bf[...] = wg_ref[0, 0].astype(BF16)
                    wu_bf[...] = wu_ref[0, 0].astype(BF16)
                    wd_bf[...] = wd_ref[0, 0].astype(BF16)

                start_gather(tok_next, q)
                start_scatter(q)
                x = xbufs[p][...].astype(BF16)
                hg = _dot(x, wg_bf[...])
                hu = _dot(x, wu_bf[...])
                hdn = (hg * jax.nn.sigmoid(hg)) * hu
                ybufs[p][...] = _dot(hdn.astype(BF16), wd_bf[...])

            @pl.when(b == n_used)
            def _():
                start_scatter(q)
                wait_scatter(q)


def _moe_experts(xn, plan, w_gate, w_up, w_down, layer):
    buf_tok, buf_dst, blk_e, n_used = plan
    n_tok, d = xn.shape
    ff = w_gate.shape[3]
    n_blocks = buf_tok.shape[0]
    last = n_blocks - 1

    def smem_blk(index):
        return pl.BlockSpec((1, 1, MOE_BLOCK), lambda b, be, nu: (index(b), 0, 0),
                            memory_space=pltpu.SMEM)

    def w_spec(shape):
        return pl.BlockSpec((1, 1) + shape,
                            lambda b, be, nu: (layer, be[jnp.minimum(b, last)], 0, 0))

    return pl.pallas_call(
        _moe_kernel,
        grid_spec=pltpu.PrefetchScalarGridSpec(
            num_scalar_prefetch=2,
            grid=(n_blocks + 1,),
            in_specs=[
                smem_blk(lambda b: jnp.minimum(b, last)),
                smem_blk(lambda b: jnp.minimum(b + 1, last)),
                smem_blk(lambda b: b),
                pl.BlockSpec(memory_space=pl.ANY),
                w_spec((d, ff)),
                w_spec((d, ff)),
                w_spec((ff, d)),
            ],
            out_specs=pl.BlockSpec(memory_space=pl.ANY),
            scratch_shapes=[
                pltpu.VMEM((MOE_BLOCK, d), F32),
                pltpu.VMEM((MOE_BLOCK, d), F32),
                pltpu.VMEM((MOE_BLOCK, d), F32),
                pltpu.VMEM((MOE_BLOCK, d), F32),
                pltpu.VMEM((d, ff), BF16),
                pltpu.VMEM((d, ff), BF16),
                pltpu.VMEM((ff, d), BF16),
                pltpu.SemaphoreType.DMA((2,)),
                pltpu.SemaphoreType.DMA((2,)),
            ],
        ),
        out_shape=jax.ShapeDtypeStruct((2 * n_tok + 2 * MOE_BLOCK, d), F32),
        compiler_params=_cparams("arbitrary"),
        name="moe_experts",
    )(blk_e, n_used, buf_tok, buf_tok, buf_dst, xn, w_gate, w_up, w_down)


def _combine_kernel(h_ref, y0_ref, y1_ref, info_ref, o_ref):
    info = info_ref[...]
    o_ref[...] = h_ref[...] + (info[:, 2:3] * y0_ref[...] + info[:, 3:4] * y1_ref[...])


def _combine(h, y2, info, *, tm=512):
    n, d = h.shape
    nt = n // tm
    return pl.pallas_call(
        _combine_kernel,
        grid=(nt,),
        in_specs=[
            pl.BlockSpec((tm, d), lambda i: (i, 0)),
            pl.BlockSpec((tm, d), lambda i: (i, 0)),
            pl.BlockSpec((tm, d), lambda i: (i + nt, 0)),
            pl.BlockSpec((tm, LANES), lambda i: (i, 0)),
        ],
        out_specs=pl.BlockSpec((tm, d), lambda i: (i, 0)),
        out_shape=jax.ShapeDtypeStruct((n, d), F32),
        compiler_params=_cparams("parallel"),
        name="moe_combine",
    )(h, y2, y2, info)


def _hier_moe_residual(h, ffn_g, w_rg, b_rg, w_re, b_re, w_gate, w_up, w_down, layer):
    xn, info = _router(h, ffn_g, w_rg, b_rg, w_re, b_re)
    plan = _dispatch_plan(info, h.shape[0])
    y2 = _moe_experts(xn, plan, w_gate, w_up, w_down, layer)
    return _combine(h, y2, info)


def _tile_gain(g, n_heads, scale=1.0):
    return jnp.tile(g.astype(F32) * scale, n_heads)


def _mem_kv(mem2d, g_mem, w_mkv, gk):
    width = N_MEM_HEADS * HEAD_DIM
    gains = jnp.concatenate([_tile_gain(gk, N_MEM_HEADS), jnp.ones((width,), F32)])
    modes = jnp.array([1, 0], jnp.int32)
    return _norm_proj(mem2d, g_mem, w_mkv.astype(BF16), gains, modes, tn=width)


def kernel(x, mem, rel_bias, attn_norm_g, ffn_norm_g, mem_norm_g, w_mem_kv, q_norm_mem, k_norm_mem, w_out, w_in_a, q_norm_a, k_norm_a, kv_norm_g, w_kv_b, k_norm_b, w_in_b, q_norm_b, lambda_b, subln_g, w_router_group, b_router_group, w_router_expert, b_router_expert, w_gate, w_up, w_down):
    bsz, seq, dm = x.shape
    n_tok = bsz * seq
    n_mem = mem.shape[1]
    d = HEAD_DIM
    mem_width = N_MEM_HEADS * d
    mix_width = dm - mem_width
    moba_heads = mix_width // d
    diff_heads = mix_width // (2 * d)
    qk_w = diff_heads * d
    tn = 512

    h = x.reshape(n_tok, dm)
    mem2d = mem.reshape(bsz * n_mem, dm)
    bias_tiles = _near_bias_tiles(rel_bias)

    gains = jnp.concatenate([
        _tile_gain(q_norm_a[0], moba_heads, ATTN_SCALE),
        _tile_gain(k_norm_a[0], moba_heads),
        jnp.ones((mix_width,), F32),
        _tile_gain(q_norm_mem[0], N_MEM_HEADS, ATTN_SCALE)])
    modes = jnp.array([1] * (2 * mix_width // tn) + [0] * (mix_width // tn) + [1] * (mem_width // tn),
                      jnp.int32)
    proj = _norm_proj(h, attn_norm_g[0], w_in_a[0].astype(BF16), gains, modes, tn=tn)
    proj = proj.reshape(bsz, seq, -1)
    kvm = _mem_kv(mem2d, mem_norm_g[0], w_mem_kv[0], k_norm_mem[0]).reshape(bsz, n_mem, -1)
    o_mix = _moba_attention(proj, bias_tiles, n_heads=moba_heads, q_off=0, k_off=moba_heads,
                            v_off=2 * moba_heads)
    o_mem = _mem_attention(proj, kvm, q_off=3 * moba_heads)
    h = _out_proj(o_mix.reshape(n_tok, -1), o_mem.reshape(n_tok, -1), w_out[0].astype(BF16), h)
    h = _hier_moe_residual(h, ffn_norm_g[0], w_router_group[0], b_router_group[0],
                           w_router_expert[0], b_router_expert[0], w_gate, w_up, w_down, 0)

    gains = jnp.concatenate([
        _tile_gain(k_norm_b[0], diff_heads), _tile_gain(k_norm_b[1], diff_heads),
        jnp.ones((diff_heads * DIFF_VDIM,), F32)])
    modes = jnp.array([1] * (2 * qk_w // tn) + [0] * (diff_heads * DIFF_VDIM // tn), jnp.int32)
    kv = _norm_proj(h, kv_norm_g, w_kv_b.astype(BF16), gains, modes, tn=tn).reshape(bsz, seq, -1)

    layer = 1
    lam_init = 0.8 - 0.6 * math.exp(-0.3 * layer)
    lam_vec = lambda_b[0].astype(F32)
    lam = (jnp.exp(jnp.sum(lam_vec[0] * lam_vec[1])) - jnp.exp(jnp.sum(lam_vec[2] * lam_vec[3]))
           + lam_init)
    gains = jnp.concatenate([
        _tile_gain(q_norm_b[0, 0], diff_heads, ATTN_SCALE),
        _tile_gain(q_norm_b[0, 1], diff_heads, ATTN_SCALE),
        _tile_gain(q_norm_mem[1], N_MEM_HEADS, ATTN_SCALE)])
    modes = jnp.ones((dm // tn,), jnp.int32)
    proj = _norm_proj(h, attn_norm_g[1], w_in_b[0].astype(BF16), gains, modes, tn=tn)
    proj = proj.reshape(bsz, seq, -1)
    kvm = _mem_kv(mem2d, mem_norm_g[1], w_mem_kv[1], k_norm_mem[1]).reshape(bsz, n_mem, -1)
    diff_bias = bias_tiles.reshape(diff_heads, 2, 2, MOBA_BLOCK, MOBA_BLOCK)
    o_mix = _diff_attention(proj, kv, diff_bias, lam, subln_g[0], n_heads=diff_heads,
                            out_scale=1.0 - lam_init)
    o_mem = _mem_attention(proj, kvm, q_off=2 * diff_heads)
    h = _out_proj(o_mix.reshape(n_tok, -1), o_mem.reshape(n_tok, -1), w_out[1].astype(BF16), h)
    h = _hier_moe_residual(h, ffn_norm_g[1], w_router_group[1], b_router_group[1],
                           w_router_expert[1], b_router_expert[1], w_gate, w_up, w_down, 1)
    return h.reshape(bsz, seq, dm)
```

```python
import functools
import math

import jax
import jax.numpy as jnp
from jax import lax
from jax.experimental import pallas as pl
from jax.experimental.pallas import tpu as pltpu

F32 = jnp.float32
BF16 = jnp.bfloat16

HEAD_DIM = 128
LANES = 128
N_MEM_HEADS = 4
MOBA_BLOCK = 256
MOBA_TOPK = 3
ATTN_TQ = 4 * MOBA_BLOCK
DIFF_VDIM = 2 * HEAD_DIM
REL_BUCKETS = 32
REL_MAX_DIST = 128
N_GROUPS = 8
EXPERTS_PER_GROUP = 8
N_EXPERTS = N_GROUPS * EXPERTS_PER_GROUP
MOE_BLOCK = 256
RMS_EPS = 1e-6
ATTN_SCALE = HEAD_DIM ** -0.5
MASKED = -1e30
VMEM_LIMIT = 56 * 1024 * 1024


def _cparams(*sem):
    return pltpu.CompilerParams(dimension_semantics=sem, vmem_limit_bytes=VMEM_LIMIT)


def _dot_t(a, b):
    return lax.dot_general(a, b, (((1,), (1,)), ((), ())), preferred_element_type=F32)


def _dot(a, b):
    return jnp.dot(a, b, preferred_element_type=F32)


def _norm_proj_kernel(mode_ref, h_ref, g_ref, w_ref, gain_ref, o_ref, xn_ref):
    j = pl.program_id(1)

    @pl.when(j == 0)
    def _():
        x = h_ref[...]
        ms = jnp.mean(x * x, axis=-1, keepdims=True)
        xn_ref[...] = (x * lax.rsqrt(ms + RMS_EPS) * g_ref[...]).astype(BF16)

    acc = _dot(xn_ref[...], w_ref[...])
    n_heads = acc.shape[1] // HEAD_DIM

    @pl.when(mode_ref[j] == 1)
    def _():
        for hh in range(n_heads):
            sl = slice(hh * HEAD_DIM, (hh + 1) * HEAD_DIM)
            a = acc[:, sl]
            ms = jnp.mean(a * a, axis=-1, keepdims=True)
            o_ref[:, sl] = (a * lax.rsqrt(ms + RMS_EPS) * gain_ref[:, sl]).astype(o_ref.dtype)

    @pl.when(mode_ref[j] != 1)
    def _():
        o_ref[...] = acc.astype(o_ref.dtype)


def _norm_proj(h, g, w, gains, modes, *, tn=512):
    n, d = h.shape
    n_out = w.shape[1]
    tm = min(1024, n)
    grid = (n // tm, n_out // tn)
    return pl.pallas_call(
        _norm_proj_kernel,
        grid_spec=pltpu.PrefetchScalarGridSpec(
            num_scalar_prefetch=1,
            grid=grid,
            in_specs=[
                pl.BlockSpec((tm, d), lambda i, j, m: (i, 0)),
                pl.BlockSpec((1, d), lambda i, j, m: (0, 0)),
                pl.BlockSpec((d, tn), lambda i, j, m: (0, j)),
                pl.BlockSpec((1, tn), lambda i, j, m: (0, j)),
            ],
            out_specs=pl.BlockSpec((tm, tn), lambda i, j, m: (i, j)),
            scratch_shapes=[pltpu.VMEM((tm, d), BF16)],
        ),
        out_shape=jax.ShapeDtypeStruct((n, n_out), BF16),
        compiler_params=_cparams("parallel", "arbitrary"),
        name="norm_proj",
    )(modes, h, g.reshape(1, d).astype(F32), w, gains.reshape(1, n_out).astype(F32))


def _t5_bucket(dist):
    n = jnp.maximum(dist, 0)
    max_exact = REL_BUCKETS // 2
    large = max_exact + (jnp.log(jnp.maximum(n, 1).astype(F32) / max_exact)
                         / math.log(REL_MAX_DIST / max_exact)
                         * (REL_BUCKETS - max_exact)).astype(jnp.int32)
    large = jnp.minimum(large, REL_BUCKETS - 1)
    return jnp.where(n < max_exact, n, large)


def _near_bias_tiles(rel_bias):
    t = MOBA_BLOCK
    tab = rel_bias.T.astype(F32)
    by_dist = tab[:, _t5_bucket(jnp.arange(2 * t))]
    far = tab[:, REL_BUCKETS - 1][:, None, None]
    r = jnp.arange(t)[:, None]
    c = jnp.arange(t)[None, :]
    d0 = r - c
    own = jnp.where(d0 >= 0, by_dist[:, jnp.maximum(d0, 0)] - far, MASKED)
    prev = by_dist[:, d0 + t] - far
    return jnp.stack([own, prev], axis=1)


def _flash_update(m_ref, l_ref, acc_ref, idx, s, v):
    m_old = m_ref[idx]
    m_new = jnp.maximum(m_old, jnp.max(s, axis=-1, keepdims=True))
    alpha = jnp.exp(m_old - m_new)
    p = jnp.exp(s - m_new)
    l_ref[idx] = alpha * l_ref[idx] + jnp.sum(p, axis=-1, keepdims=True)
    acc_ref[idx] = alpha * acc_ref[idx] + _dot(p.astype(BF16), v)
    m_ref[idx] = m_new


def _flash_init(m_ref, l_ref, acc_ref):
    m_ref[...] = jnp.full(m_ref.shape, -3e38, F32)
    l_ref[...] = jnp.zeros_like(l_ref)
    acc_ref[...] = jnp.zeros_like(acc_ref)


def _add_corner(s, corner):
    w = s.shape[1] - MOBA_BLOCK
    return jnp.concatenate([s[:, :w], s[:, w:] + corner], axis=1)


def _add_near_bias(s, a, prev_tile, own_tile):
    t = MOBA_BLOCK
    pieces = []
    if a >= 2:
        pieces.append(s[:, :(a - 1) * t])
    if a >= 1:
        pieces.append(s[:, (a - 1) * t:a * t] + prev_tile)
    pieces.append(s[:, a * t:] + own_tile)
    return pieces[0] if len(pieces) == 1 else jnp.concatenate(pieces, axis=1)


def _moba_kernel(q_ref, k_ref, v_ref, bias_ref, o_ref,
                 kmean_ref, qaug_ref, kaug_ref, m_ref, l_ref, acc_ref, *, nb):
    ti = pl.program_id(2)
    t = MOBA_BLOCK
    tq = ATTN_TQ
    n_sub = tq // t
    every = slice(None)

    @pl.when(ti == 0)
    def _():
        kmean_ref[...] = jnp.zeros_like(kmean_ref)
        for jb in range(nb):
            kb = k_ref[0, jb * t:(jb + 1) * t, :].astype(F32)
            kmean_ref[jb:jb + 1, :] = jnp.mean(kb, axis=0, keepdims=True)

    km = kmean_ref[...]
    km_hi = km.astype(BF16)
    km_lo = (km - km_hi.astype(F32)).astype(BF16)
    lane_i = lax.broadcasted_iota(jnp.int32, (t, LANES), 1)
    lane = lane_i.astype(F32)
    neg = jnp.float32(-3e38)
    for a in range(n_sub):
        rows = slice(a * t, (a + 1) * t)
        q = q_ref[0, rows, :]
        own = ti * n_sub + a
        gate = _dot_t(q, km_hi) + _dot_t(q, km_lo)
        gcur = jnp.where(lane_i < own, gate, neg)
        sel = lane_i == own
        for _ in range(MOBA_TOPK):
            mx = jnp.max(gcur, axis=-1, keepdims=True)
            idx = jnp.min(jnp.where(gcur == mx, lane, float(LANES)), axis=-1, keepdims=True)
            hit = lane == idx
            sel = sel | (hit & (mx > 0.5 * neg))
            gcur = jnp.where(hit, neg, gcur)
        pen = jnp.where(sel, 0.0, MASKED).astype(BF16)
        qaug_ref[rows, :] = jnp.concatenate([q, pen], axis=1)

    _flash_init(m_ref, l_ref, acc_ref)

    def build_kaug(c):
        start = pl.multiple_of(c * tq, tq)
        kb = k_ref[0, pl.ds(start, tq), :]
        blk = c * n_sub + lax.broadcasted_iota(jnp.int32, (tq, LANES), 0) // t
        onehot = (lax.broadcasted_iota(jnp.int32, (tq, LANES), 1) == blk).astype(BF16)
        kaug_ref[...] = jnp.concatenate([kb, onehot], axis=1)
        return start

    def far_chunk(c, carry):
        start = build_kaug(c)
        v = v_ref[0, pl.ds(start, tq), :]
        for a in range(n_sub):
            rows = slice(a * t, (a + 1) * t)
            s = _dot_t(qaug_ref[rows, :], kaug_ref[...])
            if a == 0:
                s = _add_corner(s, jnp.where(c == ti - 1, bias_ref[0, 1], 0.0))
            _flash_update(m_ref, l_ref, acc_ref, (rows, every), s, v)
        return carry

    lax.fori_loop(0, ti, far_chunk, 0)

    start = build_kaug(ti)
    for a in range(n_sub):
        rows = slice(a * t, (a + 1) * t)
        nk = (a + 1) * t
        s = _dot_t(qaug_ref[rows, :], kaug_ref[:nk, :])
        s = _add_near_bias(s, a, bias_ref[0, 1], bias_ref[0, 0])
        _flash_update(m_ref, l_ref, acc_ref, (rows, every), s, v_ref[0, pl.ds(start, nk), :])

    o_ref[0] = (acc_ref[...] / l_ref[...]).astype(o_ref.dtype)


def _moba_attention(proj, bias_tiles, *, n_heads, q_off, k_off, v_off):
    b, s, _ = proj.shape
    t = MOBA_BLOCK
    tq = ATTN_TQ
    nb = s // t
    d = HEAD_DIM
    return pl.pallas_call(
        functools.partial(_moba_kernel, nb=nb),
        grid=(b, n_heads, s // tq),
        in_specs=[
            pl.BlockSpec((1, tq, d), lambda bb, h, i: (bb, i, q_off + h)),
            pl.BlockSpec((1, s, d), lambda bb, h, i: (bb, 0, k_off + h)),
            pl.BlockSpec((1, s, d), lambda bb, h, i: (bb, 0, v_off + h)),
            pl.BlockSpec((1, 2, t, t), lambda bb, h, i: (h, 0, 0, 0)),
        ],
        out_specs=pl.BlockSpec((1, tq, d), lambda bb, h, i: (bb, i, h)),
        out_shape=jax.ShapeDtypeStruct((b, s, n_heads * d), BF16),
        scratch_shapes=[
            pltpu.VMEM((LANES, d), F32),
            pltpu.VMEM((tq, d + LANES), BF16),
            pltpu.VMEM((tq, d + LANES), BF16),
            pltpu.VMEM((tq, 1), F32),
            pltpu.VMEM((tq, 1), F32),
            pltpu.VMEM((tq, d), F32),
        ],
        compiler_params=_cparams("parallel", "parallel", "arbitrary"),
        name="moba_attention",
    )(proj, proj, proj, bias_tiles)


def _diff_kernel(lam_ref, q1_ref, q2_ref, k1_ref, k2_ref, v_ref, bias_ref, g_ref, o_ref,
                 m_ref, l_ref, acc_ref, *, out_scale):
    ti = pl.program_id(2)
    t = MOBA_BLOCK
    tq = ATTN_TQ
    n_sub = tq // t
    every = slice(None)
    q_refs = (q1_ref, q2_ref)
    k_refs = (k1_ref, k2_ref)

    _flash_init(m_ref, l_ref, acc_ref)

    def far_chunk(c, carry):
        start = pl.multiple_of(c * tq, tq)
        v = v_ref[0, pl.ds(start, tq), :]
        for mi in range(2):
            k = k_refs[mi][0, pl.ds(start, tq), :]
            for a in range(n_sub):
                rows = slice(a * t, (a + 1) * t)
                s = _dot_t(q_refs[mi][0, rows, :], k)
                if a == 0:
                    s = _add_corner(s, jnp.where(c == ti - 1, bias_ref[0, mi, 1], 0.0))
                _flash_update(m_ref, l_ref, acc_ref, (mi, rows, every), s, v)
        return carry

    lax.fori_loop(0, ti, far_chunk, 0)

    start = pl.multiple_of(ti * tq, tq)
    for mi in range(2):
        for a in range(n_sub):
            rows = slice(a * t, (a + 1) * t)
            nk = (a + 1) * t
            s = _dot_t(q_refs[mi][0, rows, :], k_refs[mi][0, pl.ds(start, nk), :])
            s = _add_near_bias(s, a, bias_ref[0, mi, 1], bias_ref[0, mi, 0])
            _flash_update(m_ref, l_ref, acc_ref, (mi, rows, every), s,
                          v_ref[0, pl.ds(start, nk), :])

    o = acc_ref[0] / l_ref[0] - lam_ref[0] * (acc_ref[1] / l_ref[1])
    ms = jnp.mean(o * o, axis=-1, keepdims=True)
    o_ref[0] = (o * lax.rsqrt(ms + RMS_EPS) * g_ref[...] * out_scale).astype(o_ref.dtype)


def _diff_attention(proj, kv, bias_tiles, lam, subln_g, *, n_heads, out_scale):
    b, s, _ = proj.shape
    t = MOBA_BLOCK
    tq = ATTN_TQ
    d = HEAD_DIM
    dv = DIFF_VDIM
    return pl.pallas_call(
        functools.partial(_diff_kernel, out_scale=out_scale),
        grid=(b, n_heads, s // tq),
        in_specs=[
            pl.BlockSpec(memory_space=pltpu.SMEM),
            pl.BlockSpec((1, tq, d), lambda bb, h, i: (bb, i, h)),
            pl.BlockSpec((1, tq, d), lambda bb, h, i: (bb, i, n_heads + h)),
            pl.BlockSpec((1, s, d), lambda bb, h, i: (bb, 0, h)),
            pl.BlockSpec((1, s, d), lambda bb, h, i: (bb, 0, n_heads + h)),
            pl.BlockSpec((1, s, dv), lambda bb, h, i: (bb, 0, n_heads + h)),
            pl.BlockSpec((1, 2, 2, t, t), lambda bb, h, i: (h, 0, 0, 0, 0)),
            pl.BlockSpec((1, dv), lambda bb, h, i: (0, 0)),
        ],
        out_specs=pl.BlockSpec((1, tq, dv), lambda bb, h, i: (bb, i, h)),
        out_shape=jax.ShapeDtypeStruct((b, s, n_heads * dv), BF16),
        scratch_shapes=[
            pltpu.VMEM((2, tq, 1), F32),
            pltpu.VMEM((2, tq, 1), F32),
            pltpu.VMEM((2, tq, dv), F32),
        ],
        compiler_params=_cparams("parallel", "parallel", "arbitrary"),
        name="diff_attention",
    )(lam.reshape(1).astype(F32), proj, proj, kv, kv, kv, bias_tiles,
      subln_g.reshape(1, dv).astype(F32))


def _mem_attn_kernel(q_ref, k_ref, v_ref, o_ref):
    s = _dot_t(q_ref[0], k_ref[0])
    m = jnp.max(s, axis=-1, keepdims=True)
    p = jnp.exp(s - m)
    l = jnp.sum(p, axis=-1, keepdims=True)
    o_ref[0] = (_dot(p.astype(BF16), v_ref[0]) / l).astype(o_ref.dtype)


def _mem_attention(proj, kvm, *, q_off):
    b, s, _ = proj.shape
    n_mem = kvm.shape[1]
    d = HEAD_DIM
    tq = min(1024, s)
    return pl.pallas_call(
        _mem_attn_kernel,
        grid=(b, N_MEM_HEADS, s // tq),
        in_specs=[
            pl.BlockSpec((1, tq, d), lambda bb, h, i: (bb, i, q_off + h)),
            pl.BlockSpec((1, n_mem, d), lambda bb, h, i: (bb, 0, h)),
            pl.BlockSpec((1, n_mem, d), lambda bb, h, i: (bb, 0, N_MEM_HEADS + h)),
        ],
        out_specs=pl.BlockSpec((1, tq, d), lambda bb, h, i: (bb, i, h)),
        out_shape=jax.ShapeDtypeStruct((b, s, N_MEM_HEADS * d), BF16),
        compiler_params=_cparams("parallel", "parallel", "arbitrary"),
        name="mem_attention",
    )(proj, kvm, kvm)


def _out_proj_kernel(a_ref, b_ref, wa_ref, wb_ref, h_ref, o_ref):
    o_ref[...] = h_ref[...] + _dot(a_ref[...], wa_ref[...]) + _dot(b_ref[...], wb_ref[...])


def _out_proj(o_mix, o_mem, w_out, h, *, tm=512, tn=2048):
    n, d = h.shape
    wa = o_mix.shape[1]
    wb = o_mem.shape[1]
    return pl.pallas_call(
        _out_proj_kernel,
        grid=(n // tm, d // tn),
        in_specs=[
            pl.BlockSpec((tm, wa), lambda i, j: (i, 0)),
            pl.BlockSpec((tm, wb), lambda i, j: (i, 0)),
            pl.BlockSpec((wa, tn), lambda i, j: (0, j)),
            pl.BlockSpec((wb, tn), lambda i, j: (0, j)),
            pl.BlockSpec((tm, tn), lambda i, j: (i, j)),
        ],
        out_specs=pl.BlockSpec((tm, tn), lambda i, j: (i, j)),
        out_shape=jax.ShapeDtypeStruct((n, d), F32),
        compiler_params=_cparams("parallel", "arbitrary"),
        name="out_proj",
    )(o_mix, o_mem, w_out[:wa], w_out[wa:], h)


def _router_kernel(h_ref, g_ref, wh_ref, wl_ref, b_ref, xn_ref, info_ref):
    x = h_ref[...]
    ms = jnp.mean(x * x, axis=-1, keepdims=True)
    xn = x * lax.rsqrt(ms + RMS_EPS) * g_ref[...]
    xn_ref[...] = xn
    xh = xn.astype(BF16)
    xl = (xn - xh.astype(F32)).astype(BF16)
    logits = _dot(xh, wh_ref[...]) + _dot(xl, wh_ref[...]) + _dot(xh, wl_ref[...]) + b_ref[...]

    tm = logits.shape[0]
    lane_i = lax.broadcasted_iota(jnp.int32, (tm, LANES), 1)
    lane = lane_i.astype(F32)
    neg = jnp.float32(-3e38)
    big = float(LANES)

    def top1(vals):
        mx = jnp.max(vals, axis=-1, keepdims=True)
        idx = jnp.min(jnp.where(vals == mx, lane, big), axis=-1, keepdims=True)
        return mx, idx

    is_group = lane_i < N_GROUPS
    gl = jnp.where(is_group, logits, neg)
    gmax, gidx = top1(gl)
    p_g = 1.0 / jnp.sum(jnp.where(is_group, jnp.exp(logits - gmax), 0.0), axis=-1, keepdims=True)

    e_lo = N_GROUPS + gidx * EXPERTS_PER_GROUP
    member = (lane >= e_lo) & (lane < e_lo + EXPERTS_PER_GROUP)
    el = jnp.where(member, logits, neg)
    v1, i1 = top1(el)
    v2, i2 = top1(jnp.where(lane == i1, neg, el))
    w2 = jnp.exp(v2 - v1)
    w1 = 1.0 / (1.0 + w2)
    info = jnp.where(lane_i == 0, i1 - N_GROUPS, 0.0)
    info = jnp.where(lane_i == 1, i2 - N_GROUPS, info)
    info = jnp.where(lane_i == 2, p_g * w1, info)
    info = jnp.where(lane_i == 3, p_g * (w2 * w1), info)
    info_ref[...] = info


def _router(h, g, w_rg, b_rg, w_re, b_re, *, tm=512):
    n, d = h.shape
    pad = LANES - N_GROUPS - N_EXPERTS
    w = jnp.concatenate([w_rg, w_re, jnp.zeros((d, pad), F32)], axis=1).astype(F32)
    bias = jnp.concatenate([b_rg, b_re, jnp.zeros((pad,), F32)]).astype(F32).reshape(1, LANES)
    w_hi = w.astype(BF16)
    w_lo = (w - w_hi.astype(F32)).astype(BF16)
    return pl.pallas_call(
        _router_kernel,
        grid=(n // tm,),
        in_specs=[
            pl.BlockSpec((tm, d), lambda i: (i, 0)),
            pl.BlockSpec((1, d), lambda i: (0, 0)),
            pl.BlockSpec((d, LANES), lambda i: (0, 0)),
            pl.BlockSpec((d, LANES), lambda i: (0, 0)),
            pl.BlockSpec((1, LANES), lambda i: (0, 0)),
        ],
        out_specs=[
            pl.BlockSpec((tm, d), lambda i: (i, 0)),
            pl.BlockSpec((tm, LANES), lambda i: (i, 0)),
        ],
        out_shape=[jax.ShapeDtypeStruct((n, d), F32), jax.ShapeDtypeStruct((n, LANES), F32)],
        compiler_params=_cparams("parallel"),
        name="moe_router",
    )(h, g.reshape(1, d).astype(F32), w_hi, w_lo, bias)


def _rank_kernel(info_ref, pstart_ref, dest_ref, run_ref):
    i = pl.program_id(0)

    @pl.when(i == 0)
    def _():
        run_ref[...] = jnp.zeros_like(run_ref)

    info = info_ref[...]
    tm = info.shape[0]
    lane = lax.broadcasted_iota(jnp.int32, (tm, LANES), 1).astype(F32)
    oh1 = lane == info[:, 0:1]
    oh2 = lane == info[:, 1:2]
    both = (oh1 | oh2).astype(BF16)
    earlier = (lax.broadcasted_iota(jnp.int32, (tm, tm), 1)
               < lax.broadcasted_iota(jnp.int32, (tm, tm), 0)).astype(BF16)
    base = _dot(earlier, both) + run_ref[...] + pstart_ref[...]
    d1 = jnp.sum(jnp.where(oh1, base, 0.0), axis=-1, keepdims=True)
    d2 = jnp.sum(jnp.where(oh2, base, 0.0), axis=-1, keepdims=True)
    run_ref[...] += jnp.sum(both.astype(F32), axis=0, keepdims=True)
    lane_i = lax.broadcasted_iota(jnp.int32, (tm, LANES), 1)
    dest = jnp.where(lane_i == 0, d1, jnp.where(lane_i == 1, d2, 0.0))
    dest_ref[...] = dest.astype(jnp.int32)


def _dispatch_plan(info, n_tok, *, tm=512):
    n_asg = 2 * n_tok
    n_blocks = -(-n_asg // MOE_BLOCK) + N_EXPERTS
    e12 = info[:, :2].astype(jnp.int32)
    onehot = e12[:, :, None] == jnp.arange(N_EXPERTS, dtype=jnp.int32)[None, None, :]
    counts = jnp.sum(onehot, axis=(0, 1), dtype=jnp.int32)
    padded = (counts + MOE_BLOCK - 1) // MOE_BLOCK * MOE_BLOCK
    pends = jnp.cumsum(padded)
    pstarts = pends - padded
    blk_e = jnp.minimum(
        jnp.sum(pends[None, :] <= (jnp.arange(n_blocks, dtype=jnp.int32) * MOE_BLOCK)[:, None],
                axis=1), N_EXPERTS - 1).astype(jnp.int32)
    n_used = (pends[-1] // MOE_BLOCK).astype(jnp.int32).reshape(1)
    pstart_row = jnp.zeros((1, LANES), F32).at[0, :N_EXPERTS].set(pstarts.astype(F32))
    dest = pl.pallas_call(
        _rank_kernel,
        grid=(n_tok // tm,),
        in_specs=[
            pl.BlockSpec((tm, LANES), lambda i: (i, 0)),
            pl.BlockSpec((1, LANES), lambda i: (0, 0)),
        ],
        out_specs=pl.BlockSpec((tm, LANES), lambda i: (i, 0)),
        out_shape=jax.ShapeDtypeStruct((n_tok, LANES), jnp.int32),
        scratch_shapes=[pltpu.VMEM((1, LANES), F32)],
        compiler_params=_cparams("arbitrary"),
        name="moe_rank",
    )(info, pstart_row)
    dest = dest[:, :2].reshape(n_tok // MOE_BLOCK, 1, 2 * MOE_BLOCK)
    tail = jnp.where(padded > counts, pends - MOE_BLOCK, -1)
    spare = n_blocks - 1 - jnp.arange(N_EXPERTS, dtype=jnp.int32)
    spare = jnp.where(spare >= n_used[0], spare * MOE_BLOCK, -1)
    zero_start = jnp.concatenate([tail, spare]).astype(jnp.int32)
    return dest, blk_e, n_used, zero_start


def _scatter_rows_kernel(zero_start_ref, dest_ref, x_hbm, xs_hbm, zeros_ref, sem):
    i = pl.program_id(0)
    nt = pl.num_programs(0) - 1
    rows = MOE_BLOCK

    def row_copy(p, r, t, d):
        return pltpu.make_async_copy(x_hbm.at[pl.ds(t, 1), :], xs_hbm.at[pl.ds(d, 1), :], sem.at[p])

    def zero_copy(start):
        start = pl.multiple_of(start, rows)
        return pltpu.make_async_copy(zeros_ref, xs_hbm.at[pl.ds(start, rows), :], sem.at[2])

    @pl.when(i == 0)
    def _():
        zeros_ref[...] = jnp.zeros_like(zeros_ref)
        for k in range(zero_start_ref.shape[0]):
            @pl.when(zero_start_ref[k] >= 0)
            def _():
                zero_copy(zero_start_ref[k]).start()
        for k in range(zero_start_ref.shape[0]):
            @pl.when(zero_start_ref[k] >= 0)
            def _():
                zero_copy(zero_start_ref[k]).wait()

    for p in (0, 1):
        @pl.when((i % 2 == p) & (i < nt))
        def _():
            base = i * rows
            for r in range(rows):
                row_copy(p, r, base + r, dest_ref[0, 0, 2 * r]).start()
                row_copy(p, r, base + r, dest_ref[0, 0, 2 * r + 1]).start()

        @pl.when((i % 2 != p) & (i >= 1))
        def _():
            for r in range(2 * rows):
                row_copy(p, 0, 0, 0).wait()


def _scatter_rows(xn, dest, zero_start, n_rows):
    n_tok, d = xn.shape
    nt = n_tok // MOE_BLOCK
    return pl.pallas_call(
        _scatter_rows_kernel,
        grid_spec=pltpu.PrefetchScalarGridSpec(
            num_scalar_prefetch=1,
            grid=(nt + 1,),
            in_specs=[
                pl.BlockSpec((1, 1, 2 * MOE_BLOCK), lambda i, pe: (jnp.minimum(i, nt - 1), 0, 0),
                             memory_space=pltpu.SMEM),
                pl.BlockSpec(memory_space=pl.ANY),
            ],
            out_specs=pl.BlockSpec(memory_space=pl.ANY),
            scratch_shapes=[pltpu.VMEM((MOE_BLOCK, d), xn.dtype), pltpu.SemaphoreType.DMA((3,))],
        ),
        out_shape=jax.ShapeDtypeStruct((n_rows, d), xn.dtype),
        compiler_params=_cparams("arbitrary"),
        name="moe_scatter_rows",
    )(zero_start, dest, xn)


def _expert_mlp_kernel(blk_e_ref, n_used_ref, xs_ref, wg_ref, wu_ref, wd_ref, ys_ref,
                       wg_bf, wu_bf, wd_bf):
    b = pl.program_id(0)

    @pl.when(b < n_used_ref[0])
    def _():
        e = blk_e_ref[b]
        e_prev = blk_e_ref[jnp.maximum(b - 1, 0)]

        @pl.when((b == 0) | (e != e_prev))
        def _():
            wg_bf[...] = wg_ref[0, 0].astype(BF16)
            wu_bf[...] = wu_ref[0, 0].astype(BF16)
            wd_bf[...] = wd_ref[0, 0].astype(BF16)

        x = xs_ref[...].astype(BF16)
        hg = _dot(x, wg_bf[...])
        hu = _dot(x, wu_bf[...])
        hdn = (hg * jax.nn.sigmoid(hg)) * hu
        ys_ref[...] = _dot(hdn.astype(BF16), wd_bf[...])

    @pl.when(b >= n_used_ref[0])
    def _():
        ys_ref[...] = jnp.zeros_like(ys_ref)


def _expert_mlp(xs, blk_e, n_used, w_gate, w_up, w_down, layer):
    n_rows, d = xs.shape
    ff = w_gate.shape[3]
    n_blocks = n_rows // MOE_BLOCK

    def used(b, nu):
        return jnp.minimum(b, nu[0] - 1)

    def w_spec(shape):
        return pl.BlockSpec((1, 1) + shape, lambda b, be, nu: (layer, be[used(b, nu)], 0, 0))

    return pl.pallas_call(
        _expert_mlp_kernel,
        grid_spec=pltpu.PrefetchScalarGridSpec(
            num_scalar_prefetch=2,
            grid=(n_blocks,),
            in_specs=[
                pl.BlockSpec((MOE_BLOCK, d), lambda b, be, nu: (used(b, nu), 0)),
                w_spec((d, ff)),
                w_spec((d, ff)),
                w_spec((ff, d)),
            ],
            out_specs=pl.BlockSpec((MOE_BLOCK, d), lambda b, be, nu: (b, 0)),
            scratch_shapes=[
                pltpu.VMEM((d, ff), BF16),
                pltpu.VMEM((d, ff), BF16),
                pltpu.VMEM((ff, d), BF16),
            ],
        ),
        out_shape=jax.ShapeDtypeStruct((n_rows, d), F32),
        compiler_params=_cparams("arbitrary"),
        name="moe_expert_mlp",
    )(blk_e, n_used, xs, w_gate, w_up, w_down)


def _combine_rows_kernel(dest_cur, dest_next, h_ref, info_ref, ys_hbm, o_ref,
                         ya0, yb0, ya1, yb1, sem):
    i = pl.program_id(0)
    nt = pl.num_programs(0)
    rows = MOE_BLOCK
    bufs = ((ya0, yb0), (ya1, yb1))

    def row_copy(p, s, r, d):
        return pltpu.make_async_copy(ys_hbm.at[pl.ds(d, 1), :], bufs[p][s].at[pl.ds(r, 1), :],
                                     sem.at[p])

    def start(dest_ref, p):
        for r in range(rows):
            row_copy(p, 0, r, dest_ref[0, 0, 2 * r]).start()
            row_copy(p, 1, r, dest_ref[0, 0, 2 * r + 1]).start()

    def wait(p):
        for r in range(rows):
            row_copy(p, 0, r, 0).wait()
            row_copy(p, 1, r, 0).wait()

    @pl.when(i == 0)
    def _():
        start(dest_cur, 0)

    for p in (0, 1):
        @pl.when(i % 2 == p)
        def _():
            @pl.when(i + 1 < nt)
            def _():
                start(dest_next, 1 - p)

            wait(p)
            info = info_ref[...]
            o_ref[...] = h_ref[...] + (info[:, 2:3] * bufs[p][0][...] + info[:, 3:4] * bufs[p][1][...])


def _combine_rows(h, ys, dest, info):
    n, d = h.shape
    tm = MOE_BLOCK
    nt = n // tm

    def dest_spec(index):
        return pl.BlockSpec((1, 1, 2 * tm), lambda i: (index(i), 0, 0), memory_space=pltpu.SMEM)

    return pl.pallas_call(
        _combine_rows_kernel,
        grid=(nt,),
        in_specs=[
            dest_spec(lambda i: i),
            dest_spec(lambda i: jnp.minimum(i + 1, nt - 1)),
            pl.BlockSpec((tm, d), lambda i: (i, 0)),
            pl.BlockSpec((tm, LANES), lambda i: (i, 0)),
            pl.BlockSpec(memory_space=pl.ANY),
        ],
        out_specs=pl.BlockSpec((tm, d), lambda i: (i, 0)),
        out_shape=jax.ShapeDtypeStruct((n, d), F32),
        scratch_shapes=[pltpu.VMEM((tm, d), F32)] * 4 + [pltpu.SemaphoreType.DMA((2,))],
        compiler_params=_cparams("arbitrary"),
        name="moe_combine_rows",
    )(dest, dest, h, info, ys)


def _hier_moe_residual(h, ffn_g, w_rg, b_rg, w_re, b_re, w_gate, w_up, w_down, layer):
    xn, info = _router(h, ffn_g, w_rg, b_rg, w_re, b_re)
    dest, blk_e, n_used, zero_start = _dispatch_plan(info, h.shape[0])
    xs = _scatter_rows(xn, dest, zero_start, blk_e.shape[0] * MOE_BLOCK)
    ys = _expert_mlp(xs, blk_e, n_used, w_gate, w_up, w_down, layer)
    return _combine_rows(h, ys, dest, info)


def _tile_gain(g, n_heads, scale=1.0):
    return jnp.tile(g.astype(F32) * scale, n_heads)


def _mem_kv(mem2d, g_mem, w_mkv, gk):
    width = N_MEM_HEADS * HEAD_DIM
    gains = jnp.concatenate([_tile_gain(gk, N_MEM_HEADS), jnp.ones((width,), F32)])
    modes = jnp.array([1, 0], jnp.int32)
    return _norm_proj(mem2d, g_mem, w_mkv.astype(BF16), gains, modes, tn=width)


def kernel(x, mem, rel_bias, attn_norm_g, ffn_norm_g, mem_norm_g, w_mem_kv, q_norm_mem, k_norm_mem, w_out, w_in_a, q_norm_a, k_norm_a, kv_norm_g, w_kv_b, k_norm_b, w_in_b, q_norm_b, lambda_b, subln_g, w_router_group, b_router_group, w_router_expert, b_router_expert, w_gate, w_up, w_down):
    bsz, seq, dm = x.shape
    n_tok = bsz * seq
    n_mem = mem.shape[1]
    d = HEAD_DIM
    mem_width = N_MEM_HEADS * d
    mix_width = dm - mem_width
    moba_heads = mix_width // d
    diff_heads = mix_width // (2 * d)
    qk_w = diff_heads * d
    tn = 512

    h = x.reshape(n_tok, dm)
    mem2d = mem.reshape(bsz * n_mem, dm)
    bias_tiles = _near_bias_tiles(rel_bias)

    gains = jnp.concatenate([
        _tile_gain(q_norm_a[0], moba_heads, ATTN_SCALE),
        _tile_gain(k_norm_a[0], moba_heads),
        jnp.ones((mix_width,), F32),
        _tile_gain(q_norm_mem[0], N_MEM_HEADS, ATTN_SCALE)])
    modes = jnp.array([1] * (2 * mix_width // tn) + [0] * (mix_width // tn) + [1] * (mem_width // tn),
                      jnp.int32)
    proj = _norm_proj(h, attn_norm_g[0], w_in_a[0].astype(BF16), gains, modes, tn=tn)
    proj = proj.reshape(bsz, seq, -1)
    kvm = _mem_kv(mem2d, mem_norm_g[0], w_mem_kv[0], k_norm_mem[0]).reshape(bsz, n_mem, -1)
    o_mix = _moba_attention(proj, bias_tiles, n_heads=moba_heads, q_off=0, k_off=moba_heads,
                            v_off=2 * moba_heads)
    o_mem = _mem_attention(proj, kvm, q_off=3 * moba_heads)
    h = _out_proj(o_mix.reshape(n_tok, -1), o_mem.reshape(n_tok, -1), w_out[0].astype(BF16), h)
    h = _hier_moe_residual(h, ffn_norm_g[0], w_router_group[0], b_router_group[0],
                           w_router_expert[0], b_router_expert[0], w_gate, w_up, w_down, 0)

    gains = jnp.concatenate([
        _tile_gain(k_norm_b[0], diff_heads), _tile_gain(k_norm_b[1], diff_heads),
        jnp.ones((diff_heads * DIFF_VDIM,), F32)])
    modes = jnp.array([1] * (2 * qk_w // tn) + [0] * (diff_heads * DIFF_VDIM // tn), jnp.int32)
    kv = _norm_proj(h, kv_norm_g, w_kv_b.astype(BF16), gains, modes, tn=tn).reshape(bsz, seq, -1)

    layer = 1
    lam_init = 0.8 - 0.6 * math.exp(-0.3 * layer)
    lam_vec = lambda_b[0].astype(F32)
    lam = (jnp.exp(jnp.sum(lam_vec[0] * lam_vec[1])) - jnp.exp(jnp.sum(lam_vec[2] * lam_vec[3]))
           + lam_init)
    gains = jnp.concatenate([
        _tile_gain(q_norm_b[0, 0], diff_heads, ATTN_SCALE),
        _tile_gain(q_norm_b[0, 1], diff_heads, ATTN_SCALE),
        _tile_gain(q_norm_mem[1], N_MEM_HEADS, ATTN_SCALE)])
    modes = jnp.ones((dm // tn,), jnp.int32)
    proj = _norm_proj(h, attn_norm_g[1], w_in_b[0].astype(BF16), gains, modes, tn=tn)
    proj = proj.reshape(bsz, seq, -1)
    kvm = _mem_kv(mem2d, mem_norm_g[1], w_mem_kv[1], k_norm_mem[1]).reshape(bsz, n_mem, -1)
    diff_bias = bias_tiles.reshape(diff_heads, 2, 2, MOBA_BLOCK, MOBA_BLOCK)
    o_mix = _diff_attention(proj, kv, diff_bias, lam, subln_g[0], n_heads=diff_heads,
                            out_scale=1.0 - lam_init)
    o_mem = _mem_attention(proj, kvm, q_off=2 * diff_heads)
    h = _out_proj(o_mix.reshape(n_tok, -1), o_mem.reshape(n_tok, -1), w_out[1].astype(BF16), h)
    h = _hier_moe_residual(h, ffn_norm_g[1], w_router_group[1], b_router_group[1],
                           w_router_expert[1], b_router_expert[1], w_gate, w_up, w_down, 1)
    return h.reshape(bsz, seq, dm)
```

```python
import functools
import math

import jax
import jax.numpy as jnp
from jax import lax
from jax.experimental import pallas as pl
from jax.experimental.pallas import tpu as pltpu

F32 = jnp.float32
BF16 = jnp.bfloat16

HEAD_DIM = 128
LANES = 128
N_MEM_HEADS = 4
MOBA_BLOCK = 256
MOBA_TOPK = 3
ATTN_TQ = 4 * MOBA_BLOCK
DIFF_VDIM = 2 * HEAD_DIM
REL_BUCKETS = 32
REL_MAX_DIST = 128
N_GROUPS = 8
EXPERTS_PER_GROUP = 8
N_EXPERTS = N_GROUPS * EXPERTS_PER_GROUP
MOE_BLOCK = 256
RMS_EPS = 1e-6
ATTN_SCALE = HEAD_DIM ** -0.5
MASKED = -1e30
VMEM_LIMIT = 56 * 1024 * 1024


def _cparams(*sem):
    return pltpu.CompilerParams(dimension_semantics=sem, vmem_limit_bytes=VMEM_LIMIT)


def _dot_t(a, b):
    return lax.dot_general(a, b, (((1,), (1,)), ((), ())), preferred_element_type=F32)


def _dot(a, b):
    return jnp.dot(a, b, preferred_element_type=F32)


def _norm_proj_kernel(mode_ref, h_ref, g_ref, w_ref, gain_ref, o_ref, xn_ref):
    j = pl.program_id(1)

    @pl.when(j == 0)
    def _():
        x = h_ref[...]
        ms = jnp.mean(x * x, axis=-1, keepdims=True)
        xn_ref[...] = (x * lax.rsqrt(ms + RMS_EPS) * g_ref[...]).astype(BF16)

    acc = _dot(xn_ref[...], w_ref[...])
    n_heads = acc.shape[1] // HEAD_DIM

    @pl.when(mode_ref[j] == 1)
    def _():
        for hh in range(n_heads):
            sl = slice(hh * HEAD_DIM, (hh + 1) * HEAD_DIM)
            a = acc[:, sl]
            ms = jnp.mean(a * a, axis=-1, keepdims=True)
            o_ref[:, sl] = (a * lax.rsqrt(ms + RMS_EPS) * gain_ref[:, sl]).astype(o_ref.dtype)

    @pl.when(mode_ref[j] != 1)
    def _():
        o_ref[...] = acc.astype(o_ref.dtype)


def _norm_proj(h, g, w, gains, modes, *, tn=512):
    n, d = h.shape
    n_out = w.shape[1]
    tm = min(1024, n)
    grid = (n // tm, n_out // tn)
    return pl.pallas_call(
        _norm_proj_kernel,
        grid_spec=pltpu.PrefetchScalarGridSpec(
            num_scalar_prefetch=1,
            grid=grid,
            in_specs=[
                pl.BlockSpec((tm, d), lambda i, j, m: (i, 0)),
                pl.BlockSpec((1, d), lambda i, j, m: (0, 0)),
                pl.BlockSpec((d, tn), lambda i, j, m: (0, j)),
                pl.BlockSpec((1, tn), lambda i, j, m: (0, j)),
            ],
            out_specs=pl.BlockSpec((tm, tn), lambda i, j, m: (i, j)),
            scratch_shapes=[pltpu.VMEM((tm, d), BF16)],
        ),
        out_shape=jax.ShapeDtypeStruct((n, n_out), BF16),
        compiler_params=_cparams("parallel", "arbitrary"),
        name="norm_proj",
    )(modes, h, g.reshape(1, d).astype(F32), w, gains.reshape(1, n_out).astype(F32))


def _t5_bucket(dist):
    n = jnp.maximum(dist, 0)
    max_exact = REL_BUCKETS // 2
    large = max_exact + (jnp.log(jnp.maximum(n, 1).astype(F32) / max_exact)
                         / math.log(REL_MAX_DIST / max_exact)
                         * (REL_BUCKETS - max_exact)).astype(jnp.int32)
    large = jnp.minimum(large, REL_BUCKETS - 1)
    return jnp.where(n < max_exact, n, large)


def _near_bias_tiles(rel_bias):
    t = MOBA_BLOCK
    n_maps = rel_bias.shape[1]
    tab = rel_bias.T.astype(F32)
    by_dist = tab[:, _t5_bucket(jnp.arange(2 * t))] - tab[:, REL_BUCKETS - 1:]
    masked = jnp.full((n_maps, t), MASKED, F32)
    own_by_diff = jnp.concatenate([by_dist[:, :1], masked, by_dist[:, t - 1:0:-1]], axis=1)
    prev_by_diff = jnp.concatenate([by_dist[:, t:0:-1], by_dist[:, :1], by_dist[:, :t:-1]], axis=1)

    def toeplitz(v):
        return jnp.tile(v, (1, t))[:, :t * (2 * t - 1)].reshape(n_maps, t, 2 * t - 1)[:, :, :t]

    return jnp.stack([toeplitz(own_by_diff), toeplitz(prev_by_diff)], axis=1)


def _flash_update(m_ref, l_ref, acc_ref, idx, s, v):
    m_old = m_ref[idx]
    m_new = jnp.maximum(m_old, jnp.max(s, axis=-1, keepdims=True))
    alpha = jnp.exp(m_old - m_new)
    p = jnp.exp(s - m_new)
    l_ref[idx] = alpha * l_ref[idx] + jnp.sum(p, axis=-1, keepdims=True)
    acc_ref[idx] = alpha * acc_ref[idx] + _dot(p.astype(BF16), v)
    m_ref[idx] = m_new


def _flash_init(m_ref, l_ref, acc_ref):
    m_ref[...] = jnp.full(m_ref.shape, -3e38, F32)
    l_ref[...] = jnp.zeros_like(l_ref)
    acc_ref[...] = jnp.zeros_like(acc_ref)


def _add_corner(s, corner):
    w = s.shape[1] - MOBA_BLOCK
    return jnp.concatenate([s[:, :w], s[:, w:] + corner], axis=1)


def _add_near_bias(s, a, prev_tile, own_tile):
    t = MOBA_BLOCK
    pieces = []
    if a >= 2:
        pieces.append(s[:, :(a - 1) * t])
    if a >= 1:
        pieces.append(s[:, (a - 1) * t:a * t] + prev_tile)
    pieces.append(s[:, a * t:] + own_tile)
    return pieces[0] if len(pieces) == 1 else jnp.concatenate(pieces, axis=1)


def _moba_kernel(q_ref, k_ref, v_ref, bias_ref, o_ref,
                 kmean_ref, qaug_ref, kaug_ref, m_ref, l_ref, acc_ref, *, nb):
    ti = pl.program_id(2)
    t = MOBA_BLOCK
    tq = ATTN_TQ
    n_sub = tq // t
    every = slice(None)

    @pl.when(ti == 0)
    def _():
        kmean_ref[...] = jnp.zeros_like(kmean_ref)
        for jb in range(nb):
            kb = k_ref[0, jb * t:(jb + 1) * t, :].astype(F32)
            kmean_ref[jb:jb + 1, :] = jnp.mean(kb, axis=0, keepdims=True)

    km = kmean_ref[...]
    km_hi = km.astype(BF16)
    km_lo = (km - km_hi.astype(F32)).astype(BF16)
    lane_i = lax.broadcasted_iota(jnp.int32, (t, LANES), 1)
    lane = lane_i.astype(F32)
    neg = jnp.float32(-3e38)
    for a in range(n_sub):
        rows = slice(a * t, (a + 1) * t)
        q = q_ref[0, rows, :]
        own = ti * n_sub + a
        gate = _dot_t(q, km_hi) + _dot_t(q, km_lo)
        gcur = jnp.where(lane_i < own, gate, neg)
        sel = lane_i == own
        for _ in range(MOBA_TOPK):
            mx = jnp.max(gcur, axis=-1, keepdims=True)
            idx = jnp.min(jnp.where(gcur == mx, lane, float(LANES)), axis=-1, keepdims=True)
            hit = lane == idx
            sel = sel | (hit & (mx > 0.5 * neg))
            gcur = jnp.where(hit, neg, gcur)
        pen = jnp.where(sel, 0.0, MASKED).astype(BF16)
        qaug_ref[rows, :] = jnp.concatenate([q, pen], axis=1)

    _flash_init(m_ref, l_ref, acc_ref)

    def build_kaug(c):
        start = pl.multiple_of(c * tq, tq)
        kb = k_ref[0, pl.ds(start, tq), :]
        blk = c * n_sub + lax.broadcasted_iota(jnp.int32, (tq, LANES), 0) // t
        onehot = (lax.broadcasted_iota(jnp.int32, (tq, LANES), 1) == blk).astype(BF16)
        kaug_ref[...] = jnp.concatenate([kb, onehot], axis=1)
        return start

    def far_chunk(c, carry):
        start = build_kaug(c)
        v = v_ref[0, pl.ds(start, tq), :]
        for a in range(n_sub):
            rows = slice(a * t, (a + 1) * t)
            s = _dot_t(qaug_ref[rows, :], kaug_ref[...])
            if a == 0:
                s = _add_corner(s, jnp.where(c == ti - 1, bias_ref[0, 1], 0.0))
            _flash_update(m_ref, l_ref, acc_ref, (rows, every), s, v)
        return carry

    lax.fori_loop(0, ti, far_chunk, 0)

    start = build_kaug(ti)
    for a in range(n_sub):
        rows = slice(a * t, (a + 1) * t)
        nk = (a + 1) * t
        s = _dot_t(qaug_ref[rows, :], kaug_ref[:nk, :])
        s = _add_near_bias(s, a, bias_ref[0, 1], bias_ref[0, 0])
        _flash_update(m_ref, l_ref, acc_ref, (rows, every), s, v_ref[0, pl.ds(start, nk), :])

    o_ref[0] = (acc_ref[...] / l_ref[...]).astype(o_ref.dtype)


def _moba_attention(proj, bias_tiles, *, n_heads, q_off, k_off, v_off):
    b, s, _ = proj.shape
    t = MOBA_BLOCK
    tq = ATTN_TQ
    nb = s // t
    d = HEAD_DIM
    return pl.pallas_call(
        functools.partial(_moba_kernel, nb=nb),
        grid=(b, n_heads, s // tq),
        in_specs=[
            pl.BlockSpec((1, tq, d), lambda bb, h, i: (bb, i, q_off + h)),
            pl.BlockSpec((1, s, d), lambda bb, h, i: (bb, 0, k_off + h)),
            pl.BlockSpec((1, s, d), lambda bb, h, i: (bb, 0, v_off + h)),
            pl.BlockSpec((1, 2, t, t), lambda bb, h, i: (h, 0, 0, 0)),
        ],
        out_specs=pl.BlockSpec((1, tq, d), lambda bb, h, i: (bb, i, h)),
        out_shape=jax.ShapeDtypeStruct((b, s, n_heads * d), BF16),
        scratch_shapes=[
            pltpu.VMEM((LANES, d), F32),
            pltpu.VMEM((tq, d + LANES), BF16),
            pltpu.VMEM((tq, d + LANES), BF16),
            pltpu.VMEM((tq, 1), F32),
            pltpu.VMEM((tq, 1), F32),
            pltpu.VMEM((tq, d), F32),
        ],
        compiler_params=_cparams("parallel", "parallel", "arbitrary"),
        name="moba_attention",
    )(proj, proj, proj, bias_tiles)


def _diff_kernel(lam_ref, q1_ref, q2_ref, k1_ref, k2_ref, v_ref, bias_ref, g_ref, o_ref,
                 m_ref, l_ref, acc_ref, *, out_scale):
    ti = pl.program_id(2)
    t = MOBA_BLOCK
    tq = ATTN_TQ
    n_sub = tq // t
    every = slice(None)
    q_refs = (q1_ref, q2_ref)
    k_refs = (k1_ref, k2_ref)

    _flash_init(m_ref, l_ref, acc_ref)

    def far_chunk(c, carry):
        start = pl.multiple_of(c * tq, tq)
        v = v_ref[0, pl.ds(start, tq), :]
        for mi in range(2):
            k = k_refs[mi][0, pl.ds(start, tq), :]
            for a in range(n_sub):
                rows = slice(a * t, (a + 1) * t)
                s = _dot_t(q_refs[mi][0, rows, :], k)
                if a == 0:
                    s = _add_corner(s, jnp.where(c == ti - 1, bias_ref[0, mi, 1], 0.0))
                _flash_update(m_ref, l_ref, acc_ref, (mi, rows, every), s, v)
        return carry

    lax.fori_loop(0, ti, far_chunk, 0)

    start = pl.multiple_of(ti * tq, tq)
    for mi in range(2):
        for a in range(n_sub):
            rows = slice(a * t, (a + 1) * t)
            nk = (a + 1) * t
            s = _dot_t(q_refs[mi][0, rows, :], k_refs[mi][0, pl.ds(start, nk), :])
            s = _add_near_bias(s, a, bias_ref[0, mi, 1], bias_ref[0, mi, 0])
            _flash_update(m_ref, l_ref, acc_ref, (mi, rows, every), s,
                          v_ref[0, pl.ds(start, nk), :])

    o = acc_ref[0] / l_ref[0] - lam_ref[0] * (acc_ref[1] / l_ref[1])
    ms = jnp.mean(o * o, axis=-1, keepdims=True)
    o_ref[0] = (o * lax.rsqrt(ms + RMS_EPS) * g_ref[...] * out_scale).astype(o_ref.dtype)


def _diff_attention(proj, kv, bias_tiles, lam, subln_g, *, n_heads, out_scale):
    b, s, _ = proj.shape
    t = MOBA_BLOCK
    tq = ATTN_TQ
    d = HEAD_DIM
    dv = DIFF_VDIM
    return pl.pallas_call(
        functools.partial(_diff_kernel, out_scale=out_scale),
        grid=(b, n_heads, s // tq),
        in_specs=[
            pl.BlockSpec(memory_space=pltpu.SMEM),
            pl.BlockSpec((1, tq, d), lambda bb, h, i: (bb, i, h)),
            pl.BlockSpec((1, tq, d), lambda bb, h, i: (bb, i, n_heads + h)),
            pl.BlockSpec((1, s, d), lambda bb, h, i: (bb, 0, h)),
            pl.BlockSpec((1, s, d), lambda bb, h, i: (bb, 0, n_heads + h)),
            pl.BlockSpec((1, s, dv), lambda bb, h, i: (bb, 0, n_heads + h)),
            pl.BlockSpec((1, 2, 2, t, t), lambda bb, h, i: (h, 0, 0, 0, 0)),
            pl.BlockSpec((1, dv), lambda bb, h, i: (0, 0)),
        ],
        out_specs=pl.BlockSpec((1, tq, dv), lambda bb, h, i: (bb, i, h)),
        out_shape=jax.ShapeDtypeStruct((b, s, n_heads * dv), BF16),
        scratch_shapes=[
            pltpu.VMEM((2, tq, 1), F32),
            pltpu.VMEM((2, tq, 1), F32),
            pltpu.VMEM((2, tq, dv), F32),
        ],
        compiler_params=_cparams("parallel", "parallel", "arbitrary"),
        name="diff_attention",
    )(lam.reshape(1).astype(F32), proj, proj, kv, kv, kv, bias_tiles,
      subln_g.reshape(1, dv).astype(F32))


def _mem_attn_kernel(q_ref, k_ref, v_ref, o_ref):
    s = _dot_t(q_ref[0], k_ref[0])
    m = jnp.max(s, axis=-1, keepdims=True)
    p = jnp.exp(s - m)
    l = jnp.sum(p, axis=-1, keepdims=True)
    o_ref[0] = (_dot(p.astype(BF16), v_ref[0]) / l).astype(o_ref.dtype)


def _mem_attention(proj, kvm, *, q_off):
    b, s, _ = proj.shape
    n_mem = kvm.shape[1]
    d = HEAD_DIM
    tq = min(1024, s)
    return pl.pallas_call(
        _mem_attn_kernel,
        grid=(b, N_MEM_HEADS, s // tq),
        in_specs=[
            pl.BlockSpec((1, tq, d), lambda bb, h, i: (bb, i, q_off + h)),
            pl.BlockSpec((1, n_mem, d), lambda bb, h, i: (bb, 0, h)),
            pl.BlockSpec((1, n_mem, d), lambda bb, h, i: (bb, 0, N_MEM_HEADS + h)),
        ],
        out_specs=pl.BlockSpec((1, tq, d), lambda bb, h, i: (bb, i, h)),
        out_shape=jax.ShapeDtypeStruct((b, s, N_MEM_HEADS * d), BF16),
        compiler_params=_cparams("parallel", "parallel", "arbitrary"),
        name="mem_attention",
    )(proj, kvm, kvm)


def _out_proj_kernel(a_ref, b_ref, wa_ref, wb_ref, h_ref, o_ref):
    o_ref[...] = h_ref[...] + _dot(a_ref[...], wa_ref[...]) + _dot(b_ref[...], wb_ref[...])


def _out_proj(o_mix, o_mem, w_out, h, *, tm=512, tn=2048):
    n, d = h.shape
    wa = o_mix.shape[1]
    wb = o_mem.shape[1]
    return pl.pallas_call(
        _out_proj_kernel,
        grid=(n // tm, d // tn),
        in_specs=[
            pl.BlockSpec((tm, wa), lambda i, j: (i, 0)),
            pl.BlockSpec((tm, wb), lambda i, j: (i, 0)),
            pl.BlockSpec((wa, tn), lambda i, j: (0, j)),
            pl.BlockSpec((wb, tn), lambda i, j: (0, j)),
            pl.BlockSpec((tm, tn), lambda i, j: (i, j)),
        ],
        out_specs=pl.BlockSpec((tm, tn), lambda i, j: (i, j)),
        out_shape=jax.ShapeDtypeStruct((n, d), F32),
        compiler_params=_cparams("parallel", "arbitrary"),
        name="out_proj",
    )(o_mix, o_mem, w_out[:wa], w_out[wa:], h)


def _pack_halves(x):
    w = x.shape[1] // 2
    bits = lax.bitcast_convert_type(x.astype(BF16).astype(F32), jnp.uint32)
    return (bits[:, :w] >> 16) | (bits[:, w:] & jnp.uint32(0xFFFF0000))


def _unpack_halves(words):
    lo = lax.bitcast_convert_type(words << 16, F32)
    hi = lax.bitcast_convert_type(words & jnp.uint32(0xFFFF0000), F32)
    return lo, hi


def _router_kernel(h_ref, g_ref, wh_ref, wl_ref, b_ref, xn_ref, info_ref):
    x = h_ref[...]
    ms = jnp.mean(x * x, axis=-1, keepdims=True)
    xn = x * lax.rsqrt(ms + RMS_EPS) * g_ref[...]
    xn_ref[...] = _pack_halves(xn)
    xh = xn.astype(BF16)
    xl = (xn - xh.astype(F32)).astype(BF16)
    logits = _dot(xh, wh_ref[...]) + _dot(xl, wh_ref[...]) + _dot(xh, wl_ref[...]) + b_ref[...]

    tm = logits.shape[0]
    lane_i = lax.broadcasted_iota(jnp.int32, (tm, LANES), 1)
    lane = lane_i.astype(F32)
    neg = jnp.float32(-3e38)
    big = float(LANES)

    def top1(vals):
        mx = jnp.max(vals, axis=-1, keepdims=True)
        idx = jnp.min(jnp.where(vals == mx, lane, big), axis=-1, keepdims=True)
        return mx, idx

    is_group = lane_i < N_GROUPS
    gl = jnp.where(is_group, logits, neg)
    gmax, gidx = top1(gl)
    p_g = 1.0 / jnp.sum(jnp.where(is_group, jnp.exp(logits - gmax), 0.0), axis=-1, keepdims=True)

    e_lo = N_GROUPS + gidx * EXPERTS_PER_GROUP
    member = (lane >= e_lo) & (lane < e_lo + EXPERTS_PER_GROUP)
    el = jnp.where(member, logits, neg)
    v1, i1 = top1(el)
    v2, i2 = top1(jnp.where(lane == i1, neg, el))
    w2 = jnp.exp(v2 - v1)
    w1 = 1.0 / (1.0 + w2)
    info = jnp.where(lane_i == 0, i1 - N_GROUPS, 0.0)
    info = jnp.where(lane_i == 1, i2 - N_GROUPS, info)
    info = jnp.where(lane_i == 2, p_g * w1, info)
    info = jnp.where(lane_i == 3, p_g * (w2 * w1), info)
    info_ref[...] = info


def _router(h, g, w_rg, b_rg, w_re, b_re, *, tm=512):
    n, d = h.shape
    pad = LANES - N_GROUPS - N_EXPERTS
    w = jnp.concatenate([w_rg, w_re, jnp.zeros((d, pad), F32)], axis=1).astype(F32)
    bias = jnp.concatenate([b_rg, b_re, jnp.zeros((pad,), F32)]).astype(F32).reshape(1, LANES)
    w_hi = w.astype(BF16)
    w_lo = (w - w_hi.astype(F32)).astype(BF16)
    return pl.pallas_call(
        _router_kernel,
        grid=(n // tm,),
        in_specs=[
            pl.BlockSpec((tm, d), lambda i: (i, 0)),
            pl.BlockSpec((1, d), lambda i: (0, 0)),
            pl.BlockSpec((d, LANES), lambda i: (0, 0)),
            pl.BlockSpec((d, LANES), lambda i: (0, 0)),
            pl.BlockSpec((1, LANES), lambda i: (0, 0)),
        ],
        out_specs=[
            pl.BlockSpec((tm, d // 2), lambda i: (i, 0)),
            pl.BlockSpec((tm, LANES), lambda i: (i, 0)),
        ],
        out_shape=[jax.ShapeDtypeStruct((n, d // 2), jnp.uint32),
                   jax.ShapeDtypeStruct((n, LANES), F32)],
        compiler_params=_cparams("parallel"),
        name="moe_router",
    )(h, g.reshape(1, d).astype(F32), w_hi, w_lo, bias)


def _rank_kernel(info_ref, pstart_ref, dest_ref, run_ref):
    i = pl.program_id(0)

    @pl.when(i == 0)
    def _():
        run_ref[...] = jnp.zeros_like(run_ref)

    info = info_ref[...]
    tm = info.shape[0]
    lane = lax.broadcasted_iota(jnp.int32, (tm, LANES), 1).astype(F32)
    oh1 = lane == info[:, 0:1]
    oh2 = lane == info[:, 1:2]
    both = (oh1 | oh2).astype(BF16)
    earlier = (lax.broadcasted_iota(jnp.int32, (tm, tm), 1)
               < lax.broadcasted_iota(jnp.int32, (tm, tm), 0)).astype(BF16)
    base = _dot(earlier, both) + run_ref[...] + pstart_ref[...]
    d1 = jnp.sum(jnp.where(oh1, base, 0.0), axis=-1, keepdims=True)
    d2 = jnp.sum(jnp.where(oh2, base, 0.0), axis=-1, keepdims=True)
    run_ref[...] += jnp.sum(both.astype(F32), axis=0, keepdims=True)
    lane_i = lax.broadcasted_iota(jnp.int32, (tm, LANES), 1)
    dest = jnp.where(lane_i == 0, d1, jnp.where(lane_i == 1, d2, 0.0))
    dest_ref[...] = dest.astype(jnp.int32)


def _dispatch_plan(info, n_tok, *, tm=512):
    n_asg = 2 * n_tok
    n_blocks = -(-n_asg // MOE_BLOCK) + N_EXPERTS
    e12 = info[:, :2].astype(jnp.int32)
    onehot = e12[:, :, None] == jnp.arange(N_EXPERTS, dtype=jnp.int32)[None, None, :]
    counts = jnp.sum(onehot, axis=(0, 1), dtype=jnp.int32)
    padded = (counts + MOE_BLOCK - 1) // MOE_BLOCK * MOE_BLOCK
    pends = jnp.cumsum(padded)
    pstarts = pends - padded
    blk_e = jnp.minimum(
        jnp.sum(pends[None, :] <= (jnp.arange(n_blocks, dtype=jnp.int32) * MOE_BLOCK)[:, None],
                axis=1), N_EXPERTS - 1).astype(jnp.int32)
    n_used = (pends[-1] // MOE_BLOCK).astype(jnp.int32).reshape(1)
    pstart_row = jnp.zeros((1, LANES), F32).at[0, :N_EXPERTS].set(pstarts.astype(F32))
    dest = pl.pallas_call(
        _rank_kernel,
        grid=(n_tok // tm,),
        in_specs=[
            pl.BlockSpec((tm, LANES), lambda i: (i, 0)),
            pl.BlockSpec((1, LANES), lambda i: (0, 0)),
        ],
        out_specs=pl.BlockSpec((tm, LANES), lambda i: (i, 0)),
        out_shape=jax.ShapeDtypeStruct((n_tok, LANES), jnp.int32),
        scratch_shapes=[pltpu.VMEM((1, LANES), F32)],
        compiler_params=_cparams("arbitrary"),
        name="moe_rank",
    )(info, pstart_row)
    dest = dest[:, :2]
    buf_tok = jnp.zeros((n_blocks * MOE_BLOCK,), jnp.int32).at[dest.reshape(-1)].set(
        jnp.arange(n_asg, dtype=jnp.int32) // 2)
    return (dest.reshape(n_tok // MOE_BLOCK, 1, 2 * MOE_BLOCK),
            buf_tok.reshape(n_blocks, 1, MOE_BLOCK), blk_e, n_used)


def _expert_mlp_kernel(blk_e_ref, n_used_ref, tok_cur, tok_next, x_hbm, wg_ref, wu_ref, wd_ref,
                       ys_ref, xb0, xb1, wg_bf, wu_bf, wd_bf, sem):
    b = pl.program_id(0)
    n_used = n_used_ref[0]
    n_blocks = pl.num_programs(0) - 1
    rows = MOE_BLOCK
    xbufs = (xb0, xb1)

    def row_copy(p, r, t):
        return pltpu.make_async_copy(x_hbm.at[pl.ds(t, 1), :], xbufs[p].at[pl.ds(r, 1), :],
                                     sem.at[p])

    def start_gather(tok_ref, p):
        for r in range(rows):
            row_copy(p, r, tok_ref[0, 0, r]).start()

    def wait_gather(p):
        for r in range(rows):
            row_copy(p, r, 0).wait()

    @pl.when(b == 0)
    def _():
        start_gather(tok_cur, 0)

    for p in (0, 1):
        @pl.when((b % 2 == p) & (b <= n_used))
        def _():
            wait_gather(p)

            @pl.when(b < n_used)
            def _():
                e = blk_e_ref[b]
                e_prev = blk_e_ref[jnp.maximum(b - 1, 0)]

                @pl.when((b == 0) | (e != e_prev))
                def _():
                    wg_bf[...] = wg_ref[0, 0].astype(BF16)
                    wu_bf[...] = wu_ref[0, 0].astype(BF16)
                    wd_bf[...] = wd_ref[0, 0].astype(BF16)

                start_gather(tok_next, 1 - p)
                x_lo, x_hi = _unpack_halves(xbufs[p][...])
                x_lo = x_lo.astype(BF16)
                x_hi = x_hi.astype(BF16)
                half = x_lo.shape[1]
                hg = _dot(x_lo, wg_bf[:half, :]) + _dot(x_hi, wg_bf[half:, :])
                hu = _dot(x_lo, wu_bf[:half, :]) + _dot(x_hi, wu_bf[half:, :])
                hdn = (hg * jax.nn.sigmoid(hg)) * hu
                ys_ref[...] = _pack_halves(_dot(hdn.astype(BF16), wd_bf[...]))

    @pl.when((b >= n_used) & (b < n_blocks))
    def _():
        ys_ref[...] = jnp.zeros_like(ys_ref)


def _expert_mlp(xn_packed, buf_tok, blk_e, n_used, w_gate, w_up, w_down, layer):
    half = xn_packed.shape[1]
    d = 2 * half
    ff = w_gate.shape[3]
    n_blocks = buf_tok.shape[0]
    last = n_blocks - 1

    def tok_spec(index):
        return pl.BlockSpec((1, 1, MOE_BLOCK), lambda b, be, nu: (index(b), 0, 0),
                            memory_space=pltpu.SMEM)

    def w_spec(shape):
        return pl.BlockSpec(
            (1, 1) + shape,
            lambda b, be, nu: (layer, be[jnp.minimum(jnp.minimum(b, nu[0] - 1), last)], 0, 0))

    return pl.pallas_call(
        _expert_mlp_kernel,
        grid_spec=pltpu.PrefetchScalarGridSpec(
            num_scalar_prefetch=2,
            grid=(n_blocks + 1,),
            in_specs=[
                tok_spec(lambda b: jnp.minimum(b, last)),
                tok_spec(lambda b: jnp.minimum(b + 1, last)),
                pl.BlockSpec(memory_space=pl.ANY),
                w_spec((d, ff)),
                w_spec((d, ff)),
                w_spec((ff, d)),
            ],
            out_specs=pl.BlockSpec((MOE_BLOCK, half), lambda b, be, nu: (jnp.minimum(b, last), 0)),
            scratch_shapes=[
                pltpu.VMEM((MOE_BLOCK, half), jnp.uint32),
                pltpu.VMEM((MOE_BLOCK, half), jnp.uint32),
                pltpu.VMEM((d, ff), BF16),
                pltpu.VMEM((d, ff), BF16),
                pltpu.VMEM((ff, d), BF16),
                pltpu.SemaphoreType.DMA((2,)),
            ],
        ),
        out_shape=jax.ShapeDtypeStruct((n_blocks * MOE_BLOCK, half), jnp.uint32),
        compiler_params=_cparams("arbitrary"),
        name="moe_expert_mlp",
    )(blk_e, n_used, buf_tok, buf_tok, xn_packed, w_gate, w_up, w_down)


def _combine_rows_kernel(dest_cur, dest_next, h_ref, info_ref, ys_hbm, o_ref,
                         ya0, yb0, ya1, yb1, sem):
    i = pl.program_id(0)
    nt = pl.num_programs(0)
    rows = MOE_BLOCK
    bufs = ((ya0, yb0), (ya1, yb1))

    def row_copy(p, s, r, d):
        return pltpu.make_async_copy(ys_hbm.at[pl.ds(d, 1), :], bufs[p][s].at[pl.ds(r, 1), :],
                                     sem.at[p])

    def start(dest_ref, p):
        for r in range(rows):
            row_copy(p, 0, r, dest_ref[0, 0, 2 * r]).start()
            row_copy(p, 1, r, dest_ref[0, 0, 2 * r + 1]).start()

    def wait(p):
        for r in range(rows):
            row_copy(p, 0, r, 0).wait()
            row_copy(p, 1, r, 0).wait()

    @pl.when(i == 0)
    def _():
        start(dest_cur, 0)

    for p in (0, 1):
        @pl.when(i % 2 == p)
        def _():
            @pl.when(i + 1 < nt)
            def _():
                start(dest_next, 1 - p)

            wait(p)
            info = info_ref[...]
            g1 = info[:, 2:3]
            g2 = info[:, 3:4]
            a_lo, a_hi = _unpack_halves(bufs[p][0][...])
            b_lo, b_hi = _unpack_halves(bufs[p][1][...])
            half = a_lo.shape[1]
            o_ref[:, :half] = h_ref[:, :half] + (g1 * a_lo + g2 * b_lo)
            o_ref[:, half:] = h_ref[:, half:] + (g1 * a_hi + g2 * b_hi)


def _combine_rows(h, ys, dest, info):
    n, d = h.shape
    tm = MOE_BLOCK
    nt = n // tm

    def dest_spec(index):
        return pl.BlockSpec((1, 1, 2 * tm), lambda i: (index(i), 0, 0), memory_space=pltpu.SMEM)

    return pl.pallas_call(
        _combine_rows_kernel,
        grid=(nt,),
        in_specs=[
            dest_spec(lambda i: i),
            dest_spec(lambda i: jnp.minimum(i + 1, nt - 1)),
            pl.BlockSpec((tm, d), lambda i: (i, 0)),
            pl.BlockSpec((tm, LANES), lambda i: (i, 0)),
            pl.BlockSpec(memory_space=pl.ANY),
        ],
        out_specs=pl.BlockSpec((tm, d), lambda i: (i, 0)),
        out_shape=jax.ShapeDtypeStruct((n, d), F32),
        scratch_shapes=([pltpu.VMEM((tm, d // 2), jnp.uint32)] * 4
                        + [pltpu.SemaphoreType.DMA((2,))]),
        compiler_params=_cparams("arbitrary"),
        name="moe_combine_rows",
    )(dest, dest, h, info, ys)


def _hier_moe_residual(h, ffn_g, w_rg, b_rg, w_re, b_re, w_gate, w_up, w_down, layer):
    xn, info = _router(h, ffn_g, w_rg, b_rg, w_re, b_re)
    dest, buf_tok, blk_e, n_used = _dispatch_plan(info, h.shape[0])
    ys = _expert_mlp(xn, buf_tok, blk_e, n_used, w_gate, w_up, w_down, layer)
    return _combine_rows(h, ys, dest, info)


def _tile_gain(g, n_heads, scale=1.0):
    return jnp.tile(g.astype(F32) * scale, n_heads)


def _mem_kv(mem2d, g_mem, w_mkv, gk):
    width = N_MEM_HEADS * HEAD_DIM
    gains = jnp.concatenate([_tile_gain(gk, N_MEM_HEADS), jnp.ones((width,), F32)])
    modes = jnp.array([1, 0], jnp.int32)
    return _norm_proj(mem2d, g_mem, w_mkv.astype(BF16), gains, modes, tn=width)


def kernel(x, mem, rel_bias, attn_norm_g, ffn_norm_g, mem_norm_g, w_mem_kv, q_norm_mem, k_norm_mem, w_out, w_in_a, q_norm_a, k_norm_a, kv_norm_g, w_kv_b, k_norm_b, w_in_b, q_norm_b, lambda_b, subln_g, w_router_group, b_router_group, w_router_expert, b_router_expert, w_gate, w_up, w_down):
    bsz, seq, dm = x.shape
    n_tok = bsz * seq
    n_mem = mem.shape[1]
    d = HEAD_DIM
    mem_width = N_MEM_HEADS * d
    mix_width = dm - mem_width
    moba_heads = mix_width // d
    diff_heads = mix_width // (2 * d)
    qk_w = diff_heads * d
    tn = 512

    h = x.reshape(n_tok, dm)
    mem2d = mem.reshape(bsz * n_mem, dm)
    bias_tiles = _near_bias_tiles(rel_bias)

    gains = jnp.concatenate([
        _tile_gain(q_norm_a[0], moba_heads, ATTN_SCALE),
        _tile_gain(k_norm_a[0], moba_heads),
        jnp.ones((mix_width,), F32),
        _tile_gain(q_norm_mem[0], N_MEM_HEADS, ATTN_SCALE)])
    modes = jnp.array([1] * (2 * mix_width // tn) + [0] * (mix_width // tn) + [1] * (mem_width // tn),
                      jnp.int32)
    proj = _norm_proj(h, attn_norm_g[0], w_in_a[0].astype(BF16), gains, modes, tn=tn)
    proj = proj.reshape(bsz, seq, -1)
    kvm = _mem_kv(mem2d, mem_norm_g[0], w_mem_kv[0], k_norm_mem[0]).reshape(bsz, n_mem, -1)
    o_mix = _moba_attention(proj, bias_tiles, n_heads=moba_heads, q_off=0, k_off=moba_heads,
                            v_off=2 * moba_heads)
    o_mem = _mem_attention(proj, kvm, q_off=3 * moba_heads)
    h = _out_proj(o_mix.reshape(n_tok, -1), o_mem.reshape(n_tok, -1), w_out[0].astype(BF16), h)
    h = _hier_moe_residual(h, ffn_norm_g[0], w_router_group[0], b_router_group[0],
                           w_router_expert[0], b_router_expert[0], w_gate, w_up, w_down, 0)

    gains = jnp.concatenate([
        _tile_gain(k_norm_b[0], diff_heads), _tile_gain(k_norm_b[1], diff_heads),
        jnp.ones((diff_heads * DIFF_VDIM,), F32)])
    modes = jnp.array([1] * (2 * qk_w // tn) + [0] * (diff_heads * DIFF_VDIM // tn), jnp.int32)
    kv = _norm_proj(h, kv_norm_g, w_kv_b.astype(BF16), gains, modes, tn=tn).reshape(bsz, seq, -1)

    layer = 1
    lam_init = 0.8 - 0.6 * math.exp(-0.3 * layer)
    lam_vec = lambda_b[0].astype(F32)
    lam = (jnp.exp(jnp.sum(lam_vec[0] * lam_vec[1])) - jnp.exp(jnp.sum(lam_vec[2] * lam_vec[3]))
           + lam_init)
    gains = jnp.concatenate([
        _tile_gain(q_norm_b[0, 0], diff_heads, ATTN_SCALE),
        _tile_gain(q_norm_b[0, 1], diff_heads, ATTN_SCALE),
        _tile_gain(q_norm_mem[1], N_MEM_HEADS, ATTN_SCALE)])
    modes = jnp.ones((dm // tn,), jnp.int32)
    proj = _norm_proj(h, attn_norm_g[1], w_in_b[0].astype(BF16), gains, modes, tn=tn)
    proj = proj.reshape(bsz, seq, -1)
    kvm = _mem_kv(mem2d, mem_norm_g[1], w_mem_kv[1], k_norm_mem[1]).reshape(bsz, n_mem, -1)
    diff_bias = bias_tiles.reshape(diff_heads, 2, 2, MOBA_BLOCK, MOBA_BLOCK)
    o_mix = _diff_attention(proj, kv, diff_bias, lam, subln_g[0], n_heads=diff_heads,
                            out_scale=1.0 - lam_init)
    o_mem = _mem_attention(proj, kvm, q_off=2 * diff_heads)
    h = _out_proj(o_mix.reshape(n_tok, -1), o_mem.reshape(n_tok, -1), w_out[1].astype(BF16), h)
    h = _hier_moe_residual(h, ffn_norm_g[1], w_router_group[1], b_router_group[1],
                           w_router_expert[1], b_router_expert[1], w_gate, w_up, w_down, 1)
    return h.reshape(bsz, seq, dm)
```

```python
import functools
import math

import jax
import jax.numpy as jnp
from jax import lax
from jax.experimental import pallas as pl
from jax.experimental.pallas import tpu as pltpu

F32 = jnp.float32
BF16 = jnp.bfloat16

HEAD_DIM = 128
LANES = 128
N_MEM_HEADS = 4
MOBA_BLOCK = 256
MOBA_TOPK = 3
ATTN_TQ = 4 * MOBA_BLOCK
DIFF_VDIM = 2 * HEAD_DIM
REL_BUCKETS = 32
REL_MAX_DIST = 128
N_GROUPS = 8
EXPERTS_PER_GROUP = 8
N_EXPERTS = N_GROUPS * EXPERTS_PER_GROUP
MOE_BLOCK = 256
RMS_EPS = 1e-6
ATTN_SCALE = HEAD_DIM ** -0.5
MASKED = -1e30
VMEM_LIMIT = 56 * 1024 * 1024


def _cparams(*sem):
    return pltpu.CompilerParams(dimension_semantics=sem, vmem_limit_bytes=VMEM_LIMIT)


def _dot_t(a, b):
    return lax.dot_general(a, b, (((1,), (1,)), ((), ())), preferred_element_type=F32)


def _dot(a, b):
    return jnp.dot(a, b, preferred_element_type=F32)


def _norm_proj_kernel(mode_ref, h_ref, g_ref, w_ref, gain_ref, o_ref, xn_ref):
    j = pl.program_id(1)

    @pl.when(j == 0)
    def _():
        x = h_ref[...]
        ms = jnp.mean(x * x, axis=-1, keepdims=True)
        xn_ref[...] = (x * lax.rsqrt(ms + RMS_EPS) * g_ref[...]).astype(BF16)

    acc = _dot(xn_ref[...], w_ref[...])
    n_heads = acc.shape[1] // HEAD_DIM

    @pl.when(mode_ref[j] == 1)
    def _():
        for hh in range(n_heads):
            sl = slice(hh * HEAD_DIM, (hh + 1) * HEAD_DIM)
            a = acc[:, sl]
            ms = jnp.mean(a * a, axis=-1, keepdims=True)
            o_ref[:, sl] = (a * lax.rsqrt(ms + RMS_EPS) * gain_ref[:, sl]).astype(o_ref.dtype)

    @pl.when(mode_ref[j] != 1)
    def _():
        o_ref[...] = acc.astype(o_ref.dtype)


def _norm_proj(h, g, w, gains, modes, *, tn=512):
    n, d = h.shape
    n_out = w.shape[1]
    tm = min(1024, n)
    grid = (n // tm, n_out // tn)
    return pl.pallas_call(
        _norm_proj_kernel,
        grid_spec=pltpu.PrefetchScalarGridSpec(
            num_scalar_prefetch=1,
            grid=grid,
            in_specs=[
                pl.BlockSpec((tm, d), lambda i, j, m: (i, 0)),
                pl.BlockSpec((1, d), lambda i, j, m: (0, 0)),
                pl.BlockSpec((d, tn), lambda i, j, m: (0, j)),
                pl.BlockSpec((1, tn), lambda i, j, m: (0, j)),
            ],
            out_specs=pl.BlockSpec((tm, tn), lambda i, j, m: (i, j)),
            scratch_shapes=[pltpu.VMEM((tm, d), BF16)],
        ),
        out_shape=jax.ShapeDtypeStruct((n, n_out), BF16),
        compiler_params=_cparams("parallel", "arbitrary"),
        name="norm_proj",
    )(modes, h, g.reshape(1, d).astype(F32), w, gains.reshape(1, n_out).astype(F32))


def _t5_bucket(dist):
    n = jnp.maximum(dist, 0)
    max_exact = REL_BUCKETS // 2
    large = max_exact + (jnp.log(jnp.maximum(n, 1).astype(F32) / max_exact)
                         / math.log(REL_MAX_DIST / max_exact)
                         * (REL_BUCKETS - max_exact)).astype(jnp.int32)
    large = jnp.minimum(large, REL_BUCKETS - 1)
    return jnp.where(n < max_exact, n, large)


def _near_bias_tiles(rel_bias):
    t = MOBA_BLOCK
    n_maps = rel_bias.shape[1]
    tab = rel_bias.T.astype(F32)
    by_dist = tab[:, _t5_bucket(jnp.arange(2 * t))] - tab[:, REL_BUCKETS - 1:]
    masked = jnp.full((n_maps, t), MASKED, F32)
    own_by_diff = jnp.concatenate([by_dist[:, :1], masked, by_dist[:, t - 1:0:-1]], axis=1)
    prev_by_diff = jnp.concatenate([by_dist[:, t:0:-1], by_dist[:, :1], by_dist[:, :t:-1]], axis=1)

    def toeplitz(v):
        return jnp.tile(v, (1, t))[:, :t * (2 * t - 1)].reshape(n_maps, t, 2 * t - 1)[:, :, :t]

    return jnp.stack([toeplitz(own_by_diff), toeplitz(prev_by_diff)], axis=1)


def _flash_update(m_ref, l_ref, acc_ref, idx, s, v):
    m_old = m_ref[idx]
    m_new = jnp.maximum(m_old, jnp.max(s, axis=-1, keepdims=True))
    alpha = jnp.exp(m_old - m_new)
    p = jnp.exp(s - m_new)
    if l_ref is not None:
        l_ref[idx] = alpha * l_ref[idx] + jnp.sum(p, axis=-1, keepdims=True)
    acc_ref[idx] = alpha * acc_ref[idx] + _dot(p.astype(BF16), v)
    m_ref[idx] = m_new


def _flash_init(m_ref, l_ref, acc_ref):
    m_ref[...] = jnp.full(m_ref.shape, -3e38, F32)
    if l_ref is not None:
        l_ref[...] = jnp.zeros_like(l_ref)
    acc_ref[...] = jnp.zeros_like(acc_ref)


def _add_corner(s, corner):
    w = s.shape[1] - MOBA_BLOCK
    return jnp.concatenate([s[:, :w], s[:, w:] + corner], axis=1)


def _add_near_bias(s, a, prev_tile, own_tile):
    t = MOBA_BLOCK
    pieces = []
    if a >= 2:
        pieces.append(s[:, :(a - 1) * t])
    if a >= 1:
        pieces.append(s[:, (a - 1) * t:a * t] + prev_tile)
    pieces.append(s[:, a * t:] + own_tile)
    return pieces[0] if len(pieces) == 1 else jnp.concatenate(pieces, axis=1)


def _moba_kernel(q_ref, k_ref, v_ref, bias_ref, o_ref,
                 kmean_ref, qaug_ref, kaug_ref, m_ref, acc_ref, *, nb):
    ti = pl.program_id(2)
    t = MOBA_BLOCK
    tq = ATTN_TQ
    n_sub = tq // t
    d = q_ref.shape[2]
    every = slice(None)

    def v_aug(start, n):
        return jnp.concatenate([v_ref[0, pl.ds(start, n), :], jnp.ones((n, LANES), BF16)], axis=1)

    @pl.when(ti == 0)
    def _():
        kmean_ref[...] = jnp.zeros_like(kmean_ref)
        for jb in range(nb):
            kb = k_ref[0, jb * t:(jb + 1) * t, :].astype(F32)
            kmean_ref[jb:jb + 1, :] = jnp.mean(kb, axis=0, keepdims=True)

    km = kmean_ref[...]
    km_hi = km.astype(BF16)
    km_lo = (km - km_hi.astype(F32)).astype(BF16)
    lane_i = lax.broadcasted_iota(jnp.int32, (t, LANES), 1)
    lane = lane_i.astype(F32)
    neg = jnp.float32(-3e38)
    for a in range(n_sub):
        rows = slice(a * t, (a + 1) * t)
        q = q_ref[0, rows, :]
        own = ti * n_sub + a
        gate = _dot_t(q, km_hi) + _dot_t(q, km_lo)
        gcur = jnp.where(lane_i < own, gate, neg)
        sel = lane_i == own
        for _ in range(MOBA_TOPK):
            mx = jnp.max(gcur, axis=-1, keepdims=True)
            idx = jnp.min(jnp.where(gcur == mx, lane, float(LANES)), axis=-1, keepdims=True)
            hit = lane == idx
            sel = sel | (hit & (mx > 0.5 * neg))
            gcur = jnp.where(hit, neg, gcur)
        pen = jnp.where(sel, 0.0, MASKED).astype(BF16)
        qaug_ref[rows, :] = jnp.concatenate([q, pen], axis=1)

    _flash_init(m_ref, None, acc_ref)

    def build_kaug(c):
        start = pl.multiple_of(c * tq, tq)
        kb = k_ref[0, pl.ds(start, tq), :]
        blk = c * n_sub + lax.broadcasted_iota(jnp.int32, (tq, LANES), 0) // t
        onehot = (lax.broadcasted_iota(jnp.int32, (tq, LANES), 1) == blk).astype(BF16)
        kaug_ref[...] = jnp.concatenate([kb, onehot], axis=1)
        return start

    def far_chunk(c, carry):
        start = build_kaug(c)
        v = v_aug(start, tq)
        for a in range(n_sub):
            rows = slice(a * t, (a + 1) * t)
            s = _dot_t(qaug_ref[rows, :], kaug_ref[...])
            if a == 0:
                s = _add_corner(s, jnp.where(c == ti - 1, bias_ref[0, 1], 0.0))
            _flash_update(m_ref, None, acc_ref, (rows, every), s, v)
        return carry

    lax.fori_loop(0, ti, far_chunk, 0)

    start = build_kaug(ti)
    for a in range(n_sub):
        rows = slice(a * t, (a + 1) * t)
        nk = (a + 1) * t
        s = _dot_t(qaug_ref[rows, :], kaug_ref[:nk, :])
        s = _add_near_bias(s, a, bias_ref[0, 1], bias_ref[0, 0])
        _flash_update(m_ref, None, acc_ref, (rows, every), s, v_aug(start, nk))

    acc = acc_ref[...]
    o_ref[0] = (acc[:, :d] / acc[:, d:]).astype(o_ref.dtype)


def _moba_attention(proj, bias_tiles, *, n_heads, q_off, k_off, v_off):
    b, s, _ = proj.shape
    t = MOBA_BLOCK
    tq = ATTN_TQ
    nb = s // t
    d = HEAD_DIM
    return pl.pallas_call(
        functools.partial(_moba_kernel, nb=nb),
        grid=(b, n_heads, s // tq),
        in_specs=[
            pl.BlockSpec((1, tq, d), lambda bb, h, i: (bb, i, q_off + h)),
            pl.BlockSpec((1, s, d), lambda bb, h, i: (bb, 0, k_off + h)),
            pl.BlockSpec((1, s, d), lambda bb, h, i: (bb, 0, v_off + h)),
            pl.BlockSpec((1, 2, t, t), lambda bb, h, i: (h, 0, 0, 0)),
        ],
        out_specs=pl.BlockSpec((1, tq, d), lambda bb, h, i: (bb, i, h)),
        out_shape=jax.ShapeDtypeStruct((b, s, n_heads * d), BF16),
        scratch_shapes=[
            pltpu.VMEM((LANES, d), F32),
            pltpu.VMEM((tq, d + LANES), BF16),
            pltpu.VMEM((tq, d + LANES), BF16),
            pltpu.VMEM((tq, 1), F32),
            pltpu.VMEM((tq, d + LANES), F32),
        ],
        compiler_params=_cparams("parallel", "parallel", "arbitrary"),
        name="moba_attention",
    )(proj, proj, proj, bias_tiles)


def _diff_kernel(lam_ref, q1_ref, q2_ref, k1_ref, k2_ref, v_ref, bias_ref, g_ref, o_ref,
                 m_ref, l_ref, acc_ref, *, out_scale):
    ti = pl.program_id(2)
    t = MOBA_BLOCK
    tq = ATTN_TQ
    n_sub = tq // t
    every = slice(None)
    q_refs = (q1_ref, q2_ref)
    k_refs = (k1_ref, k2_ref)

    _flash_init(m_ref, l_ref, acc_ref)

    def far_chunk(c, carry):
        start = pl.multiple_of(c * tq, tq)
        v = v_ref[0, pl.ds(start, tq), :]
        for mi in range(2):
            k = k_refs[mi][0, pl.ds(start, tq), :]
            for a in range(n_sub):
                rows = slice(a * t, (a + 1) * t)
                s = _dot_t(q_refs[mi][0, rows, :], k)
                if a == 0:
                    s = _add_corner(s, jnp.where(c == ti - 1, bias_ref[0, mi, 1], 0.0))
                _flash_update(m_ref, l_ref, acc_ref, (mi, rows, every), s, v)
        return carry

    lax.fori_loop(0, ti, far_chunk, 0)

    start = pl.multiple_of(ti * tq, tq)
    for mi in range(2):
        for a in range(n_sub):
            rows = slice(a * t, (a + 1) * t)
            nk = (a + 1) * t
            s = _dot_t(q_refs[mi][0, rows, :], k_refs[mi][0, pl.ds(start, nk), :])
            s = _add_near_bias(s, a, bias_ref[0, mi, 1], bias_ref[0, mi, 0])
            _flash_update(m_ref, l_ref, acc_ref, (mi, rows, every), s,
                          v_ref[0, pl.ds(start, nk), :])

    o = acc_ref[0] / l_ref[0] - lam_ref[0] * (acc_ref[1] / l_ref[1])
    ms = jnp.mean(o * o, axis=-1, keepdims=True)
    o_ref[0] = (o * lax.rsqrt(ms + RMS_EPS) * g_ref[...] * out_scale).astype(o_ref.dtype)


def _diff_attention(proj, kv, bias_tiles, lam, subln_g, *, n_heads, out_scale):
    b, s, _ = proj.shape
    t = MOBA_BLOCK
    tq = ATTN_TQ
    d = HEAD_DIM
    dv = DIFF_VDIM
    return pl.pallas_call(
        functools.partial(_diff_kernel, out_scale=out_scale),
        grid=(b, n_heads, s // tq),
        in_specs=[
            pl.BlockSpec(memory_space=pltpu.SMEM),
            pl.BlockSpec((1, tq, d), lambda bb, h, i: (bb, i, h)),
            pl.BlockSpec((1, tq, d), lambda bb, h, i: (bb, i, n_heads + h)),
            pl.BlockSpec((1, s, d), lambda bb, h, i: (bb, 0, h)),
            pl.BlockSpec((1, s, d), lambda bb, h, i: (bb, 0, n_heads + h)),
            pl.BlockSpec((1, s, dv), lambda bb, h, i: (bb, 0, n_heads + h)),
            pl.BlockSpec((1, 2, 2, t, t), lambda bb, h, i: (h, 0, 0, 0, 0)),
            pl.BlockSpec((1, dv), lambda bb, h, i: (0, 0)),
        ],
        out_specs=pl.BlockSpec((1, tq, dv), lambda bb, h, i: (bb, i, h)),
        out_shape=jax.ShapeDtypeStruct((b, s, n_heads * dv), BF16),
        scratch_shapes=[
            pltpu.VMEM((2, tq, 1), F32),
            pltpu.VMEM((2, tq, 1), F32),
            pltpu.VMEM((2, tq, dv), F32),
        ],
        compiler_params=_cparams("parallel", "parallel", "arbitrary"),
        name="diff_attention",
    )(lam.reshape(1).astype(F32), proj, proj, kv, kv, kv, bias_tiles,
      subln_g.reshape(1, dv).astype(F32))


def _mem_attn_kernel(q_ref, k_ref, v_ref, o_ref):
    s = _dot_t(q_ref[0], k_ref[0])
    m = jnp.max(s, axis=-1, keepdims=True)
    p = jnp.exp(s - m)
    l = jnp.sum(p, axis=-1, keepdims=True)
    o_ref[0] = (_dot(p.astype(BF16), v_ref[0]) / l).astype(o_ref.dtype)


def _mem_attention(proj, kvm, *, q_off):
    b, s, _ = proj.shape
    n_mem = kvm.shape[1]
    d = HEAD_DIM
    tq = min(1024, s)
    return pl.pallas_call(
        _mem_attn_kernel,
        grid=(b, N_MEM_HEADS, s // tq),
        in_specs=[
            pl.BlockSpec((1, tq, d), lambda bb, h, i: (bb, i, q_off + h)),
            pl.BlockSpec((1, n_mem, d), lambda bb, h, i: (bb, 0, h)),
            pl.BlockSpec((1, n_mem, d), lambda bb, h, i: (bb, 0, N_MEM_HEADS + h)),
        ],
        out_specs=pl.BlockSpec((1, tq, d), lambda bb, h, i: (bb, i, h)),
        out_shape=jax.ShapeDtypeStruct((b, s, N_MEM_HEADS * d), BF16),
        compiler_params=_cparams("parallel", "parallel", "arbitrary"),
        name="mem_attention",
    )(proj, kvm, kvm)


def _out_proj_kernel(a_ref, b_ref, wa_ref, wb_ref, h_ref, o_ref):
    o_ref[...] = h_ref[...] + _dot(a_ref[...], wa_ref[...]) + _dot(b_ref[...], wb_ref[...])


def _out_proj(o_mix, o_mem, w_out, h, *, tm=512, tn=2048):
    n, d = h.shape
    wa = o_mix.shape[1]
    wb = o_mem.shape[1]
    return pl.pallas_call(
        _out_proj_kernel,
        grid=(n // tm, d // tn),
        in_specs=[
            pl.BlockSpec((tm, wa), lambda i, j: (i, 0)),
            pl.BlockSpec((tm, wb), lambda i, j: (i, 0)),
            pl.BlockSpec((wa, tn), lambda i, j: (0, j)),
            pl.BlockSpec((wb, tn), lambda i, j: (0, j)),
            pl.BlockSpec((tm, tn), lambda i, j: (i, j)),
        ],
        out_specs=pl.BlockSpec((tm, tn), lambda i, j: (i, j)),
        out_shape=jax.ShapeDtypeStruct((n, d), F32),
        compiler_params=_cparams("parallel", "arbitrary"),
        name="out_proj",
    )(o_mix, o_mem, w_out[:wa], w_out[wa:], h)


def _pack_halves(x):
    w = x.shape[1] // 2
    bits = lax.bitcast_convert_type(x.astype(BF16).astype(F32), jnp.uint32)
    return (bits[:, :w] >> 16) | (bits[:, w:] & jnp.uint32(0xFFFF0000))


def _unpack_halves(words):
    lo = lax.bitcast_convert_type(words << 16, F32)
    hi = lax.bitcast_convert_type(words & jnp.uint32(0xFFFF0000), F32)
    return lo, hi


def _router_kernel(h_ref, g_ref, wh_ref, wl_ref, b_ref, xn_ref, info_ref):
    x = h_ref[...]
    ms = jnp.mean(x * x, axis=-1, keepdims=True)
    xn = x * lax.rsqrt(ms + RMS_EPS) * g_ref[...]
    xn_ref[...] = _pack_halves(xn)
    xh = xn.astype(BF16)
    xl = (xn - xh.astype(F32)).astype(BF16)
    logits = _dot(xh, wh_ref[...]) + _dot(xl, wh_ref[...]) + _dot(xh, wl_ref[...]) + b_ref[...]

    tm = logits.shape[0]
    lane_i = lax.broadcasted_iota(jnp.int32, (tm, LANES), 1)
    lane = lane_i.astype(F32)
    neg = jnp.float32(-3e38)
    big = float(LANES)

    def top1(vals):
        mx = jnp.max(vals, axis=-1, keepdims=True)
        idx = jnp.min(jnp.where(vals == mx, lane, big), axis=-1, keepdims=True)
        return mx, idx

    is_group = lane_i < N_GROUPS
    gl = jnp.where(is_group, logits, neg)
    gmax, gidx = top1(gl)
    p_g = 1.0 / jnp.sum(jnp.where(is_group, jnp.exp(logits - gmax), 0.0), axis=-1, keepdims=True)

    e_lo = N_GROUPS + gidx * EXPERTS_PER_GROUP
    member = (lane >= e_lo) & (lane < e_lo + EXPERTS_PER_GROUP)
    el = jnp.where(member, logits, neg)
    v1, i1 = top1(el)
    v2, i2 = top1(jnp.where(lane == i1, neg, el))
    w2 = jnp.exp(v2 - v1)
    w1 = 1.0 / (1.0 + w2)
    info = jnp.where(lane_i == 0, i1 - N_GROUPS, 0.0)
    info = jnp.where(lane_i == 1, i2 - N_GROUPS, info)
    info = jnp.where(lane_i == 2, p_g * w1, info)
    info = jnp.where(lane_i == 3, p_g * (w2 * w1), info)
    info_ref[...] = info


def _router(h, g, w_rg, b_rg, w_re, b_re, *, tm=512):
    n, d = h.shape
    pad = LANES - N_GROUPS - N_EXPERTS
    w = jnp.concatenate([w_rg, w_re, jnp.zeros((d, pad), F32)], axis=1).astype(F32)
    bias = jnp.concatenate([b_rg, b_re, jnp.zeros((pad,), F32)]).astype(F32).reshape(1, LANES)
    w_hi = w.astype(BF16)
    w_lo = (w - w_hi.astype(F32)).astype(BF16)
    return pl.pallas_call(
        _router_kernel,
        grid=(n // tm,),
        in_specs=[
            pl.BlockSpec((tm, d), lambda i: (i, 0)),
            pl.BlockSpec((1, d), lambda i: (0, 0)),
            pl.BlockSpec((d, LANES), lambda i: (0, 0)),
            pl.BlockSpec((d, LANES), lambda i: (0, 0)),
            pl.BlockSpec((1, LANES), lambda i: (0, 0)),
        ],
        out_specs=[
            pl.BlockSpec((tm, d // 2), lambda i: (i, 0)),
            pl.BlockSpec((tm, LANES), lambda i: (i, 0)),
        ],
        out_shape=[jax.ShapeDtypeStruct((n, d // 2), jnp.uint32),
                   jax.ShapeDtypeStruct((n, LANES), F32)],
        compiler_params=_cparams("parallel"),
        name="moe_router",
    )(h, g.reshape(1, d).astype(F32), w_hi, w_lo, bias)


def _rank_kernel(info_ref, pstart_ref, dest_ref, run_ref):
    i = pl.program_id(0)

    @pl.when(i == 0)
    def _():
        run_ref[...] = jnp.zeros_like(run_ref)

    info = info_ref[...]
    tm = info.shape[0]
    lane = lax.broadcasted_iota(jnp.int32, (tm, LANES), 1).astype(F32)
    oh1 = lane == info[:, 0:1]
    oh2 = lane == info[:, 1:2]
    both = (oh1 | oh2).astype(BF16)
    earlier = (lax.broadcasted_iota(jnp.int32, (tm, tm), 1)
               < lax.broadcasted_iota(jnp.int32, (tm, tm), 0)).astype(BF16)
    base = _dot(earlier, both) + run_ref[...] + pstart_ref[...]
    d1 = jnp.sum(jnp.where(oh1, base, 0.0), axis=-1, keepdims=True)
    d2 = jnp.sum(jnp.where(oh2, base, 0.0), axis=-1, keepdims=True)
    run_ref[...] += jnp.sum(both.astype(F32), axis=0, keepdims=True)
    lane_i = lax.broadcasted_iota(jnp.int32, (tm, LANES), 1)
    dest = jnp.where(lane_i == 0, d1, jnp.where(lane_i == 1, d2, 0.0))
    dest_ref[...] = dest.astype(jnp.int32)


def _dispatch_plan(info, n_tok, *, tm=512):
    n_asg = 2 * n_tok
    n_blocks = -(-n_asg // MOE_BLOCK) + N_EXPERTS
    e12 = info[:, :2].astype(jnp.int32)
    onehot = e12[:, :, None] == jnp.arange(N_EXPERTS, dtype=jnp.int32)[None, None, :]
    counts = jnp.sum(onehot, axis=(0, 1), dtype=jnp.int32)
    padded = (counts + MOE_BLOCK - 1) // MOE_BLOCK * MOE_BLOCK
    pends = jnp.cumsum(padded)
    pstarts = pends - padded
    blk_e = jnp.minimum(
        jnp.sum(pends[None, :] <= (jnp.arange(n_blocks, dtype=jnp.int32) * MOE_BLOCK)[:, None],
                axis=1), N_EXPERTS - 1).astype(jnp.int32)
    n_used = (pends[-1] // MOE_BLOCK).astype(jnp.int32).reshape(1)
    pstart_row = jnp.zeros((1, LANES), F32).at[0, :N_EXPERTS].set(pstarts.astype(F32))
    dest = pl.pallas_call(
        _rank_kernel,
        grid=(n_tok // tm,),
        in_specs=[
            pl.BlockSpec((tm, LANES), lambda i: (i, 0)),
            pl.BlockSpec((1, LANES), lambda i: (0, 0)),
        ],
        out_specs=pl.BlockSpec((tm, LANES), lambda i: (i, 0)),
        out_shape=jax.ShapeDtypeStruct((n_tok, LANES), jnp.int32),
        scratch_shapes=[pltpu.VMEM((1, LANES), F32)],
        compiler_params=_cparams("arbitrary"),
        name="moe_rank",
    )(info, pstart_row)
    dest = dest[:, :2]
    buf_tok = jnp.zeros((n_blocks * MOE_BLOCK,), jnp.int32).at[dest.reshape(-1)].set(
        jnp.arange(n_asg, dtype=jnp.int32) // 2)
    return (dest.reshape(n_tok // MOE_BLOCK, 1, 2 * MOE_BLOCK),
            buf_tok.reshape(n_blocks, 1, MOE_BLOCK), blk_e, n_used)


def _gather_rows_kernel(n_used_ref, tok_cur, tok_next, x_hbm, o_ref, xb0, xb1, sem):
    b = pl.program_id(0)
    n_used = n_used_ref[0]
    n_blocks = pl.num_programs(0) - 1
    rows = MOE_BLOCK
    xbufs = (xb0, xb1)

    def row_copy(p, r, t):
        return pltpu.make_async_copy(x_hbm.at[pl.ds(t, 1), :], xbufs[p].at[pl.ds(r, 1), :],
                                     sem.at[p])

    def start_gather(tok_ref, p):
        for r in range(rows):
            row_copy(p, r, tok_ref[0, 0, r]).start()

    def wait_gather(p):
        for r in range(rows):
            row_copy(p, r, 0).wait()

    @pl.when(b == 0)
    def _():
        start_gather(tok_cur, 0)

    for p in (0, 1):
        @pl.when((b % 2 == p) & (b <= n_used))
        def _():
            wait_gather(p)

            @pl.when(b < n_used)
            def _():
                start_gather(tok_next, 1 - p)
                o_ref[...] = xbufs[p][...]

    @pl.when((b >= n_used) & (b < n_blocks))
    def _():
        o_ref[...] = jnp.zeros_like(o_ref)


def _gather_rows(xn_packed, buf_tok, n_used):
    half = xn_packed.shape[1]
    n_blocks = buf_tok.shape[0]
    last = n_blocks - 1

    def tok_spec(index):
        return pl.BlockSpec((1, 1, MOE_BLOCK), lambda b, nu: (index(b), 0, 0),
                            memory_space=pltpu.SMEM)

    return pl.pallas_call(
        _gather_rows_kernel,
        grid_spec=pltpu.PrefetchScalarGridSpec(
            num_scalar_prefetch=1,
            grid=(n_blocks + 1,),
            in_specs=[
                tok_spec(lambda b: jnp.minimum(b, last)),
                tok_spec(lambda b: jnp.minimum(b + 1, last)),
                pl.BlockSpec(memory_space=pl.ANY),
            ],
            out_specs=pl.BlockSpec((MOE_BLOCK, half), lambda b, nu: (jnp.minimum(b, last), 0)),
            scratch_shapes=[
                pltpu.VMEM((MOE_BLOCK, half), jnp.uint32),
                pltpu.VMEM((MOE_BLOCK, half), jnp.uint32),
                pltpu.SemaphoreType.DMA((2,)),
            ],
        ),
        out_shape=jax.ShapeDtypeStruct((n_blocks * MOE_BLOCK, half), jnp.uint32),
        compiler_params=_cparams("arbitrary"),
        name="moe_gather_rows",
    )(n_used, buf_tok, buf_tok, xn_packed)


def _expert_mlp_kernel(blk_e_ref, n_used_ref, xs_ref, wg_ref, wu_ref, wd_ref, ys_ref,
                       wg_bf, wu_bf, wd_bf):
    b = pl.program_id(0)

    @pl.when(b < n_used_ref[0])
    def _():
        e = blk_e_ref[b]
        e_prev = blk_e_ref[jnp.maximum(b - 1, 0)]

        @pl.when((b == 0) | (e != e_prev))
        def _():
            wg_bf[...] = wg_ref[0, 0].astype(BF16)
            wu_bf[...] = wu_ref[0, 0].astype(BF16)
            wd_bf[...] = wd_ref[0, 0].astype(BF16)

        x_lo, x_hi = _unpack_halves(xs_ref[...])
        x_lo = x_lo.astype(BF16)
        x_hi = x_hi.astype(BF16)
        half = x_lo.shape[1]
        hg = _dot(x_lo, wg_bf[:half, :]) + _dot(x_hi, wg_bf[half:, :])
        hu = _dot(x_lo, wu_bf[:half, :]) + _dot(x_hi, wu_bf[half:, :])
        hdn = (hg * jax.nn.sigmoid(hg)) * hu
        ys_ref[...] = _pack_halves(_dot(hdn.astype(BF16), wd_bf[...]))

    @pl.when(b >= n_used_ref[0])
    def _():
        ys_ref[...] = jnp.zeros_like(ys_ref)


def _expert_mlp(xs, blk_e, n_used, w_gate, w_up, w_down, layer):
    n_rows, half = xs.shape
    d = 2 * half
    ff = w_gate.shape[3]
    n_blocks = n_rows // MOE_BLOCK

    def used(b, nu):
        return jnp.minimum(b, nu[0] - 1)

    def w_spec(shape):
        return pl.BlockSpec((1, 1) + shape, lambda b, be, nu: (layer, be[used(b, nu)], 0, 0))

    return pl.pallas_call(
        _expert_mlp_kernel,
        grid_spec=pltpu.PrefetchScalarGridSpec(
            num_scalar_prefetch=2,
            grid=(n_blocks,),
            in_specs=[
                pl.BlockSpec((MOE_BLOCK, half), lambda b, be, nu: (used(b, nu), 0)),
                w_spec((d, ff)),
                w_spec((d, ff)),
                w_spec((ff, d)),
            ],
            out_specs=pl.BlockSpec((MOE_BLOCK, half), lambda b, be, nu: (b, 0)),
            scratch_shapes=[
                pltpu.VMEM((d, ff), BF16),
                pltpu.VMEM((d, ff), BF16),
                pltpu.VMEM((ff, d), BF16),
            ],
        ),
        out_shape=jax.ShapeDtypeStruct((n_rows, half), jnp.uint32),
        compiler_params=_cparams("arbitrary"),
        name="moe_expert_mlp",
    )(blk_e, n_used, xs, w_gate, w_up, w_down)


def _combine_rows_kernel(dest_cur, dest_next, h_ref, info_ref, ys_hbm, o_ref,
                         ya0, yb0, ya1, yb1, sem):
    i = pl.program_id(0)
    nt = pl.num_programs(0)
    rows = MOE_BLOCK
    bufs = ((ya0, yb0), (ya1, yb1))

    def row_copy(p, s, r, d):
        return pltpu.make_async_copy(ys_hbm.at[pl.ds(d, 1), :], bufs[p][s].at[pl.ds(r, 1), :],
                                     sem.at[p])

    def start(dest_ref, p):
        for r in range(rows):
            row_copy(p, 0, r, dest_ref[0, 0, 2 * r]).start()
            row_copy(p, 1, r, dest_ref[0, 0, 2 * r + 1]).start()

    def wait(p):
        for r in range(rows):
            row_copy(p, 0, r, 0).wait()
            row_copy(p, 1, r, 0).wait()

    @pl.when(i == 0)
    def _():
        start(dest_cur, 0)

    for p in (0, 1):
        @pl.when(i % 2 == p)
        def _():
            @pl.when(i + 1 < nt)
            def _():
                start(dest_next, 1 - p)

            wait(p)
            info = info_ref[...]
            g1 = info[:, 2:3]
            g2 = info[:, 3:4]
            a_lo, a_hi = _unpack_halves(bufs[p][0][...])
            b_lo, b_hi = _unpack_halves(bufs[p][1][...])
            half = a_lo.shape[1]
            o_ref[:, :half] = h_ref[:, :half] + (g1 * a_lo + g2 * b_lo)
            o_ref[:, half:] = h_ref[:, half:] + (g1 * a_hi + g2 * b_hi)


def _combine_rows(h, ys, dest, info):
    n, d = h.shape
    tm = MOE_BLOCK
    nt = n // tm

    def dest_spec(index):
        return pl.BlockSpec((1, 1, 2 * tm), lambda i: (index(i), 0, 0), memory_space=pltpu.SMEM)

    return pl.pallas_call(
        _combine_rows_kernel,
        grid=(nt,),
        in_specs=[
            dest_spec(lambda i: i),
            dest_spec(lambda i: jnp.minimum(i + 1, nt - 1)),
            pl.BlockSpec((tm, d), lambda i: (i, 0)),
            pl.BlockSpec((tm, LANES), lambda i: (i, 0)),
            pl.BlockSpec(memory_space=pl.ANY),
        ],
        out_specs=pl.BlockSpec((tm, d), lambda i: (i, 0)),
        out_shape=jax.ShapeDtypeStruct((n, d), F32),
        scratch_shapes=([pltpu.VMEM((tm, d // 2), jnp.uint32)] * 4
                        + [pltpu.SemaphoreType.DMA((2,))]),
        compiler_params=_cparams("arbitrary"),
        name="moe_combine_rows",
    )(dest, dest, h, info, ys)


def _hier_moe_residual(h, ffn_g, w_rg, b_rg, w_re, b_re, w_gate, w_up, w_down, layer):
    xn, info = _router(h, ffn_g, w_rg, b_rg, w_re, b_re)
    dest, buf_tok, blk_e, n_used = _dispatch_plan(info, h.shape[0])
    xs = _gather_rows(xn, buf_tok, n_used)
    ys = _expert_mlp(xs, blk_e, n_used, w_gate, w_up, w_down, layer)
    return _combine_rows(h, ys, dest, info)


def _tile_gain(g, n_heads, scale=1.0):
    return jnp.tile(g.astype(F32) * scale, n_heads)


def _mem_kv(mem2d, g_mem, w_mkv, gk):
    width = N_MEM_HEADS * HEAD_DIM
    gains = jnp.concatenate([_tile_gain(gk, N_MEM_HEADS), jnp.ones((width,), F32)])
    modes = jnp.array([1, 0], jnp.int32)
    return _norm_proj(mem2d, g_mem, w_mkv.astype(BF16), gains, modes, tn=width)


def kernel(x, mem, rel_bias, attn_norm_g, ffn_norm_g, mem_norm_g, w_mem_kv, q_norm_mem, k_norm_mem, w_out, w_in_a, q_norm_a, k_norm_a, kv_norm_g, w_kv_b, k_norm_b, w_in_b, q_norm_b, lambda_b, subln_g, w_router_group, b_router_group, w_router_expert, b_router_expert, w_gate, w_up, w_down):
    bsz, seq, dm = x.shape
    n_tok = bsz * seq
    n_mem = mem.shape[1]
    d = HEAD_DIM
    mem_width = N_MEM_HEADS * d
    mix_width = dm - mem_width
    moba_heads = mix_width // d
    diff_heads = mix_width // (2 * d)
    qk_w = diff_heads * d
    tn = 512

    h = x.reshape(n_tok, dm)
    mem2d = mem.reshape(bsz * n_mem, dm)
    bias_tiles = _near_bias_tiles(rel_bias)

    gains = jnp.concatenate([
        _tile_gain(q_norm_a[0], moba_heads, ATTN_SCALE),
        _tile_gain(k_norm_a[0], moba_heads),
        jnp.ones((mix_width,), F32),
        _tile_gain(q_norm_mem[0], N_MEM_HEADS, ATTN_SCALE)])
    modes = jnp.array([1] * (2 * mix_width // tn) + [0] * (mix_width // tn) + [1] * (mem_width // tn),
                      jnp.int32)
    proj = _norm_proj(h, attn_norm_g[0], w_in_a[0].astype(BF16), gains, modes, tn=tn)
    proj = proj.reshape(bsz, seq, -1)
    kvm = _mem_kv(mem2d, mem_norm_g[0], w_mem_kv[0], k_norm_mem[0]).reshape(bsz, n_mem, -1)
    o_mix = _moba_attention(proj, bias_tiles, n_heads=moba_heads, q_off=0, k_off=moba_heads,
                            v_off=2 * moba_heads)
    o_mem = _mem_attention(proj, kvm, q_off=3 * moba_heads)
    h = _out_proj(o_mix.reshape(n_tok, -1), o_mem.reshape(n_tok, -1), w_out[0].astype(BF16), h)
    h = _hier_moe_residual(h, ffn_norm_g[0], w_router_group[0], b_router_group[0],
                           w_router_expert[0], b_router_expert[0], w_gate, w_up, w_down, 0)

    gains = jnp.concatenate([
        _tile_gain(k_norm_b[0], diff_heads), _tile_gain(k_norm_b[1], diff_heads),
        jnp.ones((diff_heads * DIFF_VDIM,), F32)])
    modes = jnp.array([1] * (2 * qk_w // tn) + [0] * (diff_heads * DIFF_VDIM // tn), jnp.int32)
    kv = _norm_proj(h, kv_norm_g, w_kv_b.astype(BF16), gains, modes, tn=tn).reshape(bsz, seq, -1)

    layer = 1
    lam_init = 0.8 - 0.6 * math.exp(-0.3 * layer)
    lam_vec = lambda_b[0].astype(F32)
    lam = (jnp.exp(jnp.sum(lam_vec[0] * lam_vec[1])) - jnp.exp(jnp.sum(lam_vec[2] * lam_vec[3]))
           + lam_init)
    gains = jnp.concatenate([
        _tile_gain(q_norm_b[0, 0], diff_heads, ATTN_SCALE),
        _tile_gain(q_norm_b[0, 1], diff_heads, ATTN_SCALE),
        _tile_gain(q_norm_mem[1], N_MEM_HEADS, ATTN_SCALE)])
    modes = jnp.ones((dm // tn,), jnp.int32)
    proj = _norm_proj(h, attn_norm_g[1], w_in_b[0].astype(BF16), gains, modes, tn=tn)
    proj = proj.reshape(bsz, seq, -1)
    kvm = _mem_kv(mem2d, mem_norm_g[1], w_mem_kv[1], k_norm_mem[1]).reshape(bsz, n_mem, -1)
    diff_bias = bias_tiles.reshape(diff_heads, 2, 2, MOBA_BLOCK, MOBA_BLOCK)
    o_mix = _diff_attention(proj, kv, diff_bias, lam, subln_g[0], n_heads=diff_heads,
                            out_scale=1.0 - lam_init)
    o_mem = _mem_attention(proj, kvm, q_off=2 * diff_heads)
    h = _out_proj(o_mix.reshape(n_tok, -1), o_mem.reshape(n_tok, -1), w_out[1].astype(BF16), h)
    h = _hier_moe_residual(h, ffn_norm_g[1], w_router_group[1], b_router_group[1],
                           w_router_expert[1], b_router_expert[1], w_gate, w_up, w_down, 1)
    return h.reshape(bsz, seq, dm)
```

```python
import functools
import math

import jax
import jax.numpy as jnp
from jax import lax
from jax.experimental import pallas as pl
from jax.experimental.pallas import tpu as pltpu

F32 = jnp.float32
BF16 = jnp.bfloat16

HEAD_DIM = 128
LANES = 128
N_MEM_HEADS = 4
MOBA_BLOCK = 256
MOBA_TOPK = 3
ATTN_TQ = 4 * MOBA_BLOCK
DIFF_VDIM = 2 * HEAD_DIM
REL_BUCKETS = 32
REL_MAX_DIST = 128
N_GROUPS = 8
EXPERTS_PER_GROUP = 8
N_EXPERTS = N_GROUPS * EXPERTS_PER_GROUP
MOE_BLOCK = 256
RMS_EPS = 1e-6
ATTN_SCALE = HEAD_DIM ** -0.5
MASKED = -1e30
VMEM_LIMIT = 56 * 1024 * 1024


def _cparams(*sem):
    return pltpu.CompilerParams(dimension_semantics=sem, vmem_limit_bytes=VMEM_LIMIT)


def _dot_t(a, b):
    return lax.dot_general(a, b, (((1,), (1,)), ((), ())), preferred_element_type=F32)


def _dot(a, b):
    return jnp.dot(a, b, preferred_element_type=F32)


def _norm_proj_kernel(mode_ref, h_ref, g_ref, w_ref, gain_ref, o_ref, xn_ref):
    j = pl.program_id(1)

    @pl.when(j == 0)
    def _():
        x = h_ref[...]
        ms = jnp.mean(x * x, axis=-1, keepdims=True)
        xn_ref[...] = (x * lax.rsqrt(ms + RMS_EPS) * g_ref[...]).astype(BF16)

    acc = _dot(xn_ref[...], w_ref[...])
    n_heads = acc.shape[1] // HEAD_DIM

    @pl.when(mode_ref[j] == 1)
    def _():
        for hh in range(n_heads):
            sl = slice(hh * HEAD_DIM, (hh + 1) * HEAD_DIM)
            a = acc[:, sl]
            ms = jnp.mean(a * a, axis=-1, keepdims=True)
            o_ref[:, sl] = (a * lax.rsqrt(ms + RMS_EPS) * gain_ref[:, sl]).astype(o_ref.dtype)

    @pl.when(mode_ref[j] != 1)
    def _():
        o_ref[...] = acc.astype(o_ref.dtype)


def _norm_proj(h, g, w, gains, modes, *, tn=512):
    n, d = h.shape
    n_out = w.shape[1]
    tm = min(1024, n)
    grid = (n // tm, n_out // tn)
    return pl.pallas_call(
        _norm_proj_kernel,
        grid_spec=pltpu.PrefetchScalarGridSpec(
            num_scalar_prefetch=1,
            grid=grid,
            in_specs=[
                pl.BlockSpec((tm, d), lambda i, j, m: (i, 0)),
                pl.BlockSpec((1, d), lambda i, j, m: (0, 0)),
                pl.BlockSpec((d, tn), lambda i, j, m: (0, j)),
                pl.BlockSpec((1, tn), lambda i, j, m: (0, j)),
            ],
            out_specs=pl.BlockSpec((tm, tn), lambda i, j, m: (i, j)),
            scratch_shapes=[pltpu.VMEM((tm, d), BF16)],
        ),
        out_shape=jax.ShapeDtypeStruct((n, n_out), BF16),
        compiler_params=_cparams("parallel", "arbitrary"),
        name="norm_proj",
    )(modes, h, g.reshape(1, d).astype(F32), w, gains.reshape(1, n_out).astype(F32))


def _t5_bucket(dist):
    n = jnp.maximum(dist, 0)
    max_exact = REL_BUCKETS // 2
    large = max_exact + (jnp.log(jnp.maximum(n, 1).astype(F32) / max_exact)
                         / math.log(REL_MAX_DIST / max_exact)
                         * (REL_BUCKETS - max_exact)).astype(jnp.int32)
    large = jnp.minimum(large, REL_BUCKETS - 1)
    return jnp.where(n < max_exact, n, large)


def _near_bias_tiles(rel_bias):
    t = MOBA_BLOCK
    n_maps = rel_bias.shape[1]
    tab = rel_bias.T.astype(F32)
    by_dist = tab[:, _t5_bucket(jnp.arange(2 * t))] - tab[:, REL_BUCKETS - 1:]
    masked = jnp.full((n_maps, t), MASKED, F32)
    own_by_diff = jnp.concatenate([by_dist[:, :1], masked, by_dist[:, t - 1:0:-1]], axis=1)
    prev_by_diff = jnp.concatenate([by_dist[:, t:0:-1], by_dist[:, :1], by_dist[:, :t:-1]], axis=1)

    def toeplitz(v):
        return jnp.tile(v, (1, t))[:, :t * (2 * t - 1)].reshape(n_maps, t, 2 * t - 1)[:, :, :t]

    return jnp.stack([toeplitz(own_by_diff), toeplitz(prev_by_diff)], axis=1)


def _flash_update(m_ref, l_ref, acc_ref, idx, s, v):
    m_old = m_ref[idx]
    m_new = jnp.maximum(m_old, jnp.max(s, axis=-1, keepdims=True))
    alpha = jnp.exp(m_old - m_new)
    p = jnp.exp(s - m_new)
    if l_ref is not None:
        l_ref[idx] = alpha * l_ref[idx] + jnp.sum(p, axis=-1, keepdims=True)
    acc_ref[idx] = alpha * acc_ref[idx] + _dot(p.astype(BF16), v)
    m_ref[idx] = m_new


def _flash_init(m_ref, l_ref, acc_ref):
    m_ref[...] = jnp.full(m_ref.shape, -3e38, F32)
    if l_ref is not None:
        l_ref[...] = jnp.zeros_like(l_ref)
    acc_ref[...] = jnp.zeros_like(acc_ref)


def _add_corner(s, corner):
    w = s.shape[1] - MOBA_BLOCK
    return jnp.concatenate([s[:, :w], s[:, w:] + corner], axis=1)


def _add_near_bias(s, a, prev_tile, own_tile):
    t = MOBA_BLOCK
    pieces = []
    if a >= 2:
        pieces.append(s[:, :(a - 1) * t])
    if a >= 1:
        pieces.append(s[:, (a - 1) * t:a * t] + prev_tile)
    pieces.append(s[:, a * t:] + own_tile)
    return pieces[0] if len(pieces) == 1 else jnp.concatenate(pieces, axis=1)


def _moba_kernel(q_ref, k_ref, v_ref, bias_ref, o_ref,
                 kmean_ref, qaug_ref, kaug_ref, m_ref, acc_ref, *, nb):
    ti = pl.program_id(2)
    t = MOBA_BLOCK
    tq = ATTN_TQ
    n_sub = tq // t
    d = q_ref.shape[2]
    every = slice(None)

    def v_aug(start, n):
        return jnp.concatenate([v_ref[0, pl.ds(start, n), :], jnp.ones((n, LANES), BF16)], axis=1)

    @pl.when(ti == 0)
    def _():
        kmean_ref[...] = jnp.zeros_like(kmean_ref)
        for jb in range(nb):
            kb = k_ref[0, jb * t:(jb + 1) * t, :].astype(F32)
            kmean_ref[jb:jb + 1, :] = jnp.mean(kb, axis=0, keepdims=True)

    km = kmean_ref[...]
    km_hi = km.astype(BF16)
    km_lo = (km - km_hi.astype(F32)).astype(BF16)
    lane_i = lax.broadcasted_iota(jnp.int32, (t, LANES), 1)
    lane = lane_i.astype(F32)
    neg = jnp.float32(-3e38)
    for a in range(n_sub):
        rows = slice(a * t, (a + 1) * t)
        q = q_ref[0, rows, :]
        own = ti * n_sub + a
        gate = _dot_t(q, km_hi) + _dot_t(q, km_lo)
        gcur = jnp.where(lane_i < own, gate, neg)
        sel = lane_i == own
        for _ in range(MOBA_TOPK):
            mx = jnp.max(gcur, axis=-1, keepdims=True)
            idx = jnp.min(jnp.where(gcur == mx, lane, float(LANES)), axis=-1, keepdims=True)
            hit = lane == idx
            sel = sel | (hit & (mx > 0.5 * neg))
            gcur = jnp.where(hit, neg, gcur)
        pen = jnp.where(sel, 0.0, MASKED).astype(BF16)
        qaug_ref[rows, :] = jnp.concatenate([q, pen], axis=1)

    _flash_init(m_ref, None, acc_ref)

    def build_kaug(c):
        start = pl.multiple_of(c * tq, tq)
        kb = k_ref[0, pl.ds(start, tq), :]
        blk = c * n_sub + lax.broadcasted_iota(jnp.int32, (tq, LANES), 0) // t
        onehot = (lax.broadcasted_iota(jnp.int32, (tq, LANES), 1) == blk).astype(BF16)
        kaug_ref[...] = jnp.concatenate([kb, onehot], axis=1)
        return start

    def far_chunk(c, carry):
        start = build_kaug(c)
        v = v_aug(start, tq)
        for a in range(n_sub):
            rows = slice(a * t, (a + 1) * t)
            s = _dot_t(qaug_ref[rows, :], kaug_ref[...])
            if a == 0:
                s = _add_corner(s, jnp.where(c == ti - 1, bias_ref[0, 1], 0.0))
            _flash_update(m_ref, None, acc_ref, (rows, every), s, v)
        return carry

    lax.fori_loop(0, ti, far_chunk, 0)

    start = build_kaug(ti)
    for a in range(n_sub):
        rows = slice(a * t, (a + 1) * t)
        nk = (a + 1) * t
        s = _dot_t(qaug_ref[rows, :], kaug_ref[:nk, :])
        s = _add_near_bias(s, a, bias_ref[0, 1], bias_ref[0, 0])
        _flash_update(m_ref, None, acc_ref, (rows, every), s, v_aug(start, nk))

    acc = acc_ref[...]
    o_ref[0] = (acc[:, :d] / acc[:, d:]).astype(o_ref.dtype)


def _moba_attention(proj, bias_tiles, *, n_heads, q_off, k_off, v_off):
    b, s, _ = proj.shape
    t = MOBA_BLOCK
    tq = ATTN_TQ
    nb = s // t
    d = HEAD_DIM
    return pl.pallas_call(
        functools.partial(_moba_kernel, nb=nb),
        grid=(b, n_heads, s // tq),
        in_specs=[
            pl.BlockSpec((1, tq, d), lambda bb, h, i: (bb, i, q_off + h)),
            pl.BlockSpec((1, s, d), lambda bb, h, i: (bb, 0, k_off + h)),
            pl.BlockSpec((1, s, d), lambda bb, h, i: (bb, 0, v_off + h)),
            pl.BlockSpec((1, 2, t, t), lambda bb, h, i: (h, 0, 0, 0)),
        ],
        out_specs=pl.BlockSpec((1, tq, d), lambda bb, h, i: (bb, i, h)),
        out_shape=jax.ShapeDtypeStruct((b, s, n_heads * d), BF16),
        scratch_shapes=[
            pltpu.VMEM((LANES, d), F32),
            pltpu.VMEM((tq, d + LANES), BF16),
            pltpu.VMEM((tq, d + LANES), BF16),
            pltpu.VMEM((tq, 1), F32),
            pltpu.VMEM((tq, d + LANES), F32),
        ],
        compiler_params=_cparams("parallel", "parallel", "arbitrary"),
        name="moba_attention",
    )(proj, proj, proj, bias_tiles)


def _diff_kernel(lam_ref, q1_ref, q2_ref, k1_ref, k2_ref, v_ref, bias_ref, g_ref, o_ref,
                 m_ref, l_ref, acc_ref, *, out_scale):
    ti = pl.program_id(2)
    t = MOBA_BLOCK
    tq = ATTN_TQ
    n_sub = tq // t
    every = slice(None)
    q_refs = (q1_ref, q2_ref)
    k_refs = (k1_ref, k2_ref)

    _flash_init(m_ref, l_ref, acc_ref)

    def far_chunk(c, carry):
        start = pl.multiple_of(c * tq, tq)
        v = v_ref[0, pl.ds(start, tq), :]
        for mi in range(2):
            k = k_refs[mi][0, pl.ds(start, tq), :]
            for a in range(n_sub):
                rows = slice(a * t, (a + 1) * t)
                s = _dot_t(q_refs[mi][0, rows, :], k)
                if a == 0:
                    s = _add_corner(s, jnp.where(c == ti - 1, bias_ref[0, mi, 1], 0.0))
                _flash_update(m_ref, l_ref, acc_ref, (mi, rows, every), s, v)
        return carry

    lax.fori_loop(0, ti, far_chunk, 0)

    start = pl.multiple_of(ti * tq, tq)
    for mi in range(2):
        for a in range(n_sub):
            rows = slice(a * t, (a + 1) * t)
            nk = (a + 1) * t
            s = _dot_t(q_refs[mi][0, rows, :], k_refs[mi][0, pl.ds(start, nk), :])
            s = _add_near_bias(s, a, bias_ref[0, mi, 1], bias_ref[0, mi, 0])
            _flash_update(m_ref, l_ref, acc_ref, (mi, rows, every), s,
                          v_ref[0, pl.ds(start, nk), :])

    o = acc_ref[0] / l_ref[0] - lam_ref[0] * (acc_ref[1] / l_ref[1])
    ms = jnp.mean(o * o, axis=-1, keepdims=True)
    o_ref[0] = (o * lax.rsqrt(ms + RMS_EPS) * g_ref[...] * out_scale).astype(o_ref.dtype)


def _diff_attention(proj, kv, bias_tiles, lam, subln_g, *, n_heads, out_scale):
    b, s, _ = proj.shape
    t = MOBA_BLOCK
    tq = ATTN_TQ
    d = HEAD_DIM
    dv = DIFF_VDIM
    return pl.pallas_call(
        functools.partial(_diff_kernel, out_scale=out_scale),
        grid=(b, n_heads, s // tq),
        in_specs=[
            pl.BlockSpec(memory_space=pltpu.SMEM),
            pl.BlockSpec((1, tq, d), lambda bb, h, i: (bb, i, h)),
            pl.BlockSpec((1, tq, d), lambda bb, h, i: (bb, i, n_heads + h)),
            pl.BlockSpec((1, s, d), lambda bb, h, i: (bb, 0, h)),
            pl.BlockSpec((1, s, d), lambda bb, h, i: (bb, 0, n_heads + h)),
            pl.BlockSpec((1, s, dv), lambda bb, h, i: (bb, 0, n_heads + h)),
            pl.BlockSpec((1, 2, 2, t, t), lambda bb, h, i: (h, 0, 0, 0, 0)),
            pl.BlockSpec((1, dv), lambda bb, h, i: (0, 0)),
        ],
        out_specs=pl.BlockSpec((1, tq, dv), lambda bb, h, i: (bb, i, h)),
        out_shape=jax.ShapeDtypeStruct((b, s, n_heads * dv), BF16),
        scratch_shapes=[
            pltpu.VMEM((2, tq, 1), F32),
            pltpu.VMEM((2, tq, 1), F32),
            pltpu.VMEM((2, tq, dv), F32),
        ],
        compiler_params=_cparams("parallel", "parallel", "arbitrary"),
        name="diff_attention",
    )(lam.reshape(1).astype(F32), proj, proj, kv, kv, kv, bias_tiles,
      subln_g.reshape(1, dv).astype(F32))


def _mem_attn_kernel(q_ref, k_ref, v_ref, o_ref):
    s = _dot_t(q_ref[0], k_ref[0])
    m = jnp.max(s, axis=-1, keepdims=True)
    p = jnp.exp(s - m)
    l = jnp.sum(p, axis=-1, keepdims=True)
    o_ref[0] = (_dot(p.astype(BF16), v_ref[0]) / l).astype(o_ref.dtype)


def _mem_attention(proj, kvm, *, q_off):
    b, s, _ = proj.shape
    n_mem = kvm.shape[1]
    d = HEAD_DIM
    tq = min(1024, s)
    return pl.pallas_call(
        _mem_attn_kernel,
        grid=(b, N_MEM_HEADS, s // tq),
        in_specs=[
            pl.BlockSpec((1, tq, d), lambda bb, h, i: (bb, i, q_off + h)),
            pl.BlockSpec((1, n_mem, d), lambda bb, h, i: (bb, 0, h)),
            pl.BlockSpec((1, n_mem, d), lambda bb, h, i: (bb, 0, N_MEM_HEADS + h)),
        ],
        out_specs=pl.BlockSpec((1, tq, d), lambda bb, h, i: (bb, i, h)),
        out_shape=jax.ShapeDtypeStruct((b, s, N_MEM_HEADS * d), BF16),
        compiler_params=_cparams("parallel", "parallel", "arbitrary"),
        name="mem_attention",
    )(proj, kvm, kvm)


def _out_proj_kernel(a_ref, b_ref, wa_ref, wb_ref, h_ref, o_ref):
    o_ref[...] = h_ref[...] + _dot(a_ref[...], wa_ref[...]) + _dot(b_ref[...], wb_ref[...])


def _out_proj(o_mix, o_mem, w_out, h, *, tm=512, tn=2048):
    n, d = h.shape
    wa = o_mix.shape[1]
    wb = o_mem.shape[1]
    return pl.pallas_call(
        _out_proj_kernel,
        grid=(n // tm, d // tn),
        in_specs=[
            pl.BlockSpec((tm, wa), lambda i, j: (i, 0)),
            pl.BlockSpec((tm, wb), lambda i, j: (i, 0)),
            pl.BlockSpec((wa, tn), lambda i, j: (0, j)),
            pl.BlockSpec((wb, tn), lambda i, j: (0, j)),
            pl.BlockSpec((tm, tn), lambda i, j: (i, j)),
        ],
        out_specs=pl.BlockSpec((tm, tn), lambda i, j: (i, j)),
        out_shape=jax.ShapeDtypeStruct((n, d), F32),
        compiler_params=_cparams("parallel", "arbitrary"),
        name="out_proj",
    )(o_mix, o_mem, w_out[:wa], w_out[wa:], h)


def _pack_halves(x):
    w = x.shape[1] // 2
    bits = lax.bitcast_convert_type(x.astype(BF16).astype(F32), jnp.uint32)
    return (bits[:, :w] >> 16) | (bits[:, w:] & jnp.uint32(0xFFFF0000))


def _unpack_halves(words):
    lo = lax.bitcast_convert_type(words << 16, F32)
    hi = lax.bitcast_convert_type(words & jnp.uint32(0xFFFF0000), F32)
    return lo, hi


def _router_kernel(h_ref, g_ref, wh_ref, wl_ref, b_ref, xn_ref, info_ref):
    x = h_ref[...]
    ms = jnp.mean(x * x, axis=-1, keepdims=True)
    xn = x * lax.rsqrt(ms + RMS_EPS) * g_ref[...]
    xn_ref[...] = _pack_halves(xn)
    xh = xn.astype(BF16)
    xl = (xn - xh.astype(F32)).astype(BF16)
    logits = _dot(xh, wh_ref[...]) + _dot(xl, wh_ref[...]) + _dot(xh, wl_ref[...]) + b_ref[...]

    tm = logits.shape[0]
    lane_i = lax.broadcasted_iota(jnp.int32, (tm, LANES), 1)
    lane = lane_i.astype(F32)
    neg = jnp.float32(-3e38)
    big = float(LANES)

    def top1(vals):
        mx = jnp.max(vals, axis=-1, keepdims=True)
        idx = jnp.min(jnp.where(vals == mx, lane, big), axis=-1, keepdims=True)
        return mx, idx

    is_group = lane_i < N_GROUPS
    gl = jnp.where(is_group, logits, neg)
    gmax, gidx = top1(gl)
    p_g = 1.0 / jnp.sum(jnp.where(is_group, jnp.exp(logits - gmax), 0.0), axis=-1, keepdims=True)

    e_lo = N_GROUPS + gidx * EXPERTS_PER_GROUP
    member = (lane >= e_lo) & (lane < e_lo + EXPERTS_PER_GROUP)
    el = jnp.where(member, logits, neg)
    v1, i1 = top1(el)
    v2, i2 = top1(jnp.where(lane == i1, neg, el))
    w2 = jnp.exp(v2 - v1)
    w1 = 1.0 / (1.0 + w2)
    info = jnp.where(lane_i == 0, i1 - N_GROUPS, 0.0)
    info = jnp.where(lane_i == 1, i2 - N_GROUPS, info)
    info = jnp.where(lane_i == 2, p_g * w1, info)
    info = jnp.where(lane_i == 3, p_g * (w2 * w1), info)
    info_ref[...] = info


def _router(h, g, w_rg, b_rg, w_re, b_re, *, tm=512):
    n, d = h.shape
    pad = LANES - N_GROUPS - N_EXPERTS
    w = jnp.concatenate([w_rg, w_re, jnp.zeros((d, pad), F32)], axis=1).astype(F32)
    bias = jnp.concatenate([b_rg, b_re, jnp.zeros((pad,), F32)]).astype(F32).reshape(1, LANES)
    w_hi = w.astype(BF16)
    w_lo = (w - w_hi.astype(F32)).astype(BF16)
    return pl.pallas_call(
        _router_kernel,
        grid=(n // tm,),
        in_specs=[
            pl.BlockSpec((tm, d), lambda i: (i, 0)),
            pl.BlockSpec((1, d), lambda i: (0, 0)),
            pl.BlockSpec((d, LANES), lambda i: (0, 0)),
            pl.BlockSpec((d, LANES), lambda i: (0, 0)),
            pl.BlockSpec((1, LANES), lambda i: (0, 0)),
        ],
        out_specs=[
            pl.BlockSpec((tm, d // 2), lambda i: (i, 0)),
            pl.BlockSpec((tm, LANES), lambda i: (i, 0)),
        ],
        out_shape=[jax.ShapeDtypeStruct((n, d // 2), jnp.uint32),
                   jax.ShapeDtypeStruct((n, LANES), F32)],
        compiler_params=_cparams("parallel"),
        name="moe_router",
    )(h, g.reshape(1, d).astype(F32), w_hi, w_lo, bias)


def _rank_kernel(info_ref, pstart_ref, dest_ref, run_ref):
    i = pl.program_id(0)

    @pl.when(i == 0)
    def _():
        run_ref[...] = jnp.zeros_like(run_ref)

    info = info_ref[...]
    tm = info.shape[0]
    lane = lax.broadcasted_iota(jnp.int32, (tm, LANES), 1).astype(F32)
    oh1 = lane == info[:, 0:1]
    oh2 = lane == info[:, 1:2]
    both = (oh1 | oh2).astype(BF16)
    earlier = (lax.broadcasted_iota(jnp.int32, (tm, tm), 1)
               < lax.broadcasted_iota(jnp.int32, (tm, tm), 0)).astype(BF16)
    base = _dot(earlier, both) + run_ref[...] + pstart_ref[...]
    d1 = jnp.sum(jnp.where(oh1, base, 0.0), axis=-1, keepdims=True)
    d2 = jnp.sum(jnp.where(oh2, base, 0.0), axis=-1, keepdims=True)
    run_ref[...] += jnp.sum(both.astype(F32), axis=0, keepdims=True)
    lane_i = lax.broadcasted_iota(jnp.int32, (tm, LANES), 1)
    dest = jnp.where(lane_i == 0, d1, jnp.where(lane_i == 1, d2, 0.0))
    dest_ref[...] = dest.astype(jnp.int32)


def _dispatch_plan(info, n_tok, *, tm=512):
    n_asg = 2 * n_tok
    n_blocks = -(-n_asg // MOE_BLOCK) + N_EXPERTS
    e12 = info[:, :2].astype(jnp.int32)
    onehot = e12[:, :, None] == jnp.arange(N_EXPERTS, dtype=jnp.int32)[None, None, :]
    counts = jnp.sum(onehot, axis=(0, 1), dtype=jnp.int32)
    padded = (counts + MOE_BLOCK - 1) // MOE_BLOCK * MOE_BLOCK
    pends = jnp.cumsum(padded)
    pstarts = pends - padded
    blk_e = jnp.minimum(
        jnp.sum(pends[None, :] <= (jnp.arange(n_blocks, dtype=jnp.int32) * MOE_BLOCK)[:, None],
                axis=1), N_EXPERTS - 1).astype(jnp.int32)
    n_used = (pends[-1] // MOE_BLOCK).astype(jnp.int32).reshape(1)
    pstart_row = jnp.zeros((1, LANES), F32).at[0, :N_EXPERTS].set(pstarts.astype(F32))
    dest = pl.pallas_call(
        _rank_kernel,
        grid=(n_tok // tm,),
        in_specs=[
            pl.BlockSpec((tm, LANES), lambda i: (i, 0)),
            pl.BlockSpec((1, LANES), lambda i: (0, 0)),
        ],
        out_specs=pl.BlockSpec((tm, LANES), lambda i: (i, 0)),
        out_shape=jax.ShapeDtypeStruct((n_tok, LANES), jnp.int32),
        scratch_shapes=[pltpu.VMEM((1, LANES), F32)],
        compiler_params=_cparams("arbitrary"),
        name="moe_rank",
    )(info, pstart_row)
    dest = dest[:, :2]
    buf_tok = jnp.zeros((n_blocks * MOE_BLOCK,), jnp.int32).at[dest.reshape(-1)].set(
        jnp.arange(n_asg, dtype=jnp.int32) // 2)
    return (dest.reshape(n_tok // MOE_BLOCK, 1, 2 * MOE_BLOCK),
            buf_tok.reshape(n_blocks, 1, MOE_BLOCK), blk_e, n_used)


def _gather_rows_kernel(n_used_ref, tok_cur, tok_next, tok_after, x_hbm, o_ref, xb0, xb1, sem):
    b = pl.program_id(0)
    n_used = n_used_ref[0]
    rows = MOE_BLOCK
    xbufs = (xb0, xb1)

    def row_copy(p, r, t):
        return pltpu.make_async_copy(x_hbm.at[pl.ds(t, 1), :], xbufs[p].at[pl.ds(r, 1), :],
                                     sem.at[p])

    def start_gather(tok_ref, p):
        for r in range(rows):
            row_copy(p, r, tok_ref[0, 0, r]).start()

    def wait_gather(p):
        for r in range(rows):
            row_copy(p, r, 0).wait()

    @pl.when(b == 0)
    def _():
        start_gather(tok_cur, 0)

        @pl.when(n_used > 1)
        def _():
            start_gather(tok_next, 1)

    for p in (0, 1):
        @pl.when((b % 2 == p) & (b < n_used))
        def _():
            wait_gather(p)
            o_ref[...] = xbufs[p][...]

            @pl.when(b + 2 < n_used)
            def _():
                start_gather(tok_after, p)

    @pl.when(b >= n_used)
    def _():
        o_ref[...] = jnp.zeros_like(o_ref)


def _gather_rows(xn_packed, buf_tok, n_used):
    half = xn_packed.shape[1]
    n_blocks = buf_tok.shape[0]
    last = n_blocks - 1

    def tok_spec(index):
        return pl.BlockSpec((1, 1, MOE_BLOCK), lambda b, nu: (index(b), 0, 0),
                            memory_space=pltpu.SMEM)

    return pl.pallas_call(
        _gather_rows_kernel,
        grid_spec=pltpu.PrefetchScalarGridSpec(
            num_scalar_prefetch=1,
            grid=(n_blocks,),
            in_specs=[
                tok_spec(lambda b: b),
                tok_spec(lambda b: jnp.minimum(b + 1, last)),
                tok_spec(lambda b: jnp.minimum(b + 2, last)),
                pl.BlockSpec(memory_space=pl.ANY),
            ],
            out_specs=pl.BlockSpec((MOE_BLOCK, half), lambda b, nu: (b, 0)),
            scratch_shapes=[
                pltpu.VMEM((MOE_BLOCK, half), jnp.uint32),
                pltpu.VMEM((MOE_BLOCK, half), jnp.uint32),
                pltpu.SemaphoreType.DMA((2,)),
            ],
        ),
        out_shape=jax.ShapeDtypeStruct((n_blocks * MOE_BLOCK, half), jnp.uint32),
        compiler_params=_cparams("arbitrary"),
        name="moe_gather_rows",
    )(n_used, buf_tok, buf_tok, buf_tok, xn_packed)


def _expert_mlp_kernel(blk_e_ref, n_used_ref, xs_ref, wg_ref, wu_ref, wd_ref, ys_ref,
                       wg_bf, wu_bf, wd_bf):
    b = pl.program_id(0)

    @pl.when(b < n_used_ref[0])
    def _():
        e = blk_e_ref[b]
        e_prev = blk_e_ref[jnp.maximum(b - 1, 0)]

        @pl.when((b == 0) | (e != e_prev))
        def _():
            wg_bf[...] = wg_ref[0, 0].astype(BF16)
            wu_bf[...] = wu_ref[0, 0].astype(BF16)
            wd_bf[...] = wd_ref[0, 0].astype(BF16)

        x_lo, x_hi = _unpack_halves(xs_ref[...])
        x_lo = x_lo.astype(BF16)
        x_hi = x_hi.astype(BF16)
        half = x_lo.shape[1]
        hg = _dot(x_lo, wg_bf[:half, :]) + _dot(x_hi, wg_bf[half:, :])
        hu = _dot(x_lo, wu_bf[:half, :]) + _dot(x_hi, wu_bf[half:, :])
        hdn = (hg * jax.nn.sigmoid(hg)) * hu
        ys_ref[...] = _pack_halves(_dot(hdn.astype(BF16), wd_bf[...]))

    @pl.when(b >= n_used_ref[0])
    def _():
        ys_ref[...] = jnp.zeros_like(ys_ref)


def _expert_mlp(xs, blk_e, n_used, w_gate, w_up, w_down, layer):
    n_rows, half = xs.shape
    d = 2 * half
    ff = w_gate.shape[3]
    n_blocks = n_rows // MOE_BLOCK

    def used(b, nu):
        return jnp.minimum(b, nu[0] - 1)

    def w_spec(shape):
        return pl.BlockSpec((1, 1) + shape, lambda b, be, nu: (layer, be[used(b, nu)], 0, 0))

    return pl.pallas_call(
        _expert_mlp_kernel,
        grid_spec=pltpu.PrefetchScalarGridSpec(
            num_scalar_prefetch=2,
            grid=(n_blocks,),
            in_specs=[
                pl.BlockSpec((MOE_BLOCK, half), lambda b, be, nu: (used(b, nu), 0)),
                w_spec((d, ff)),
                w_spec((d, ff)),
                w_spec((ff, d)),
            ],
            out_specs=pl.BlockSpec((MOE_BLOCK, half), lambda b, be, nu: (b, 0)),
            scratch_shapes=[
                pltpu.VMEM((d, ff), BF16),
                pltpu.VMEM((d, ff), BF16),
                pltpu.VMEM((ff, d), BF16),
            ],
        ),
        out_shape=jax.ShapeDtypeStruct((n_rows, half), jnp.uint32),
        compiler_params=_cparams("arbitrary"),
        name="moe_expert_mlp",
    )(blk_e, n_used, xs, w_gate, w_up, w_down)


def _combine_rows_kernel(dest_cur, dest_next, h_ref, info_ref, ys_hbm, o_ref,
                         ya0, yb0, ya1, yb1, sem):
    i = pl.program_id(0)
    nt = pl.num_programs(0)
    rows = MOE_BLOCK
    bufs = ((ya0, yb0), (ya1, yb1))

    def row_copy(p, s, r, d):
        return pltpu.make_async_copy(ys_hbm.at[pl.ds(d, 1), :], bufs[p][s].at[pl.ds(r, 1), :],
                                     sem.at[p])

    def start(dest_ref, p):
        for r in range(rows):
            row_copy(p, 0, r, dest_ref[0, 0, 2 * r]).start()
            row_copy(p, 1, r, dest_ref[0, 0, 2 * r + 1]).start()

    def wait(p):
        for r in range(rows):
            row_copy(p, 0, r, 0).wait()
            row_copy(p, 1, r, 0).wait()

    @pl.when(i == 0)
    def _():
        start(dest_cur, 0)

    for p in (0, 1):
        @pl.when(i % 2 == p)
        def _():
            @pl.when(i + 1 < nt)
            def _():
                start(dest_next, 1 - p)

            wait(p)
            info = info_ref[...]
            g1 = info[:, 2:3]
            g2 = info[:, 3:4]
            a_lo, a_hi = _unpack_halves(bufs[p][0][...])
            b_lo, b_hi = _unpack_halves(bufs[p][1][...])
            half = a_lo.shape[1]
            o_ref[:, :half] = h_ref[:, :half] + (g1 * a_lo + g2 * b_lo)
            o_ref[:, half:] = h_ref[:, half:] + (g1 * a_hi + g2 * b_hi)


def _combine_rows(h, ys, dest, info):
    n, d = h.shape
    tm = MOE_BLOCK
    nt = n // tm

    def dest_spec(index):
        return pl.BlockSpec((1, 1, 2 * tm), lambda i: (index(i), 0, 0), memory_space=pltpu.SMEM)

    return pl.pallas_call(
        _combine_rows_kernel,
        grid=(nt,),
        in_specs=[
            dest_spec(lambda i: i),
            dest_spec(lambda i: jnp.minimum(i + 1, nt - 1)),
            pl.BlockSpec((tm, d), lambda i: (i, 0)),
            pl.BlockSpec((tm, LANES), lambda i: (i, 0)),
            pl.BlockSpec(memory_space=pl.ANY),
        ],
        out_specs=pl.BlockSpec((tm, d), lambda i: (i, 0)),
        out_shape=jax.ShapeDtypeStruct((n, d), F32),
        scratch_shapes=([pltpu.VMEM((tm, d // 2), jnp.uint32)] * 4
                        + [pltpu.SemaphoreType.DMA((2,))]),
        compiler_params=_cparams("arbitrary"),
        name="moe_combine_rows",
    )(dest, dest, h, info, ys)


def _hier_moe_residual(h, ffn_g, w_rg, b_rg, w_re, b_re, w_gate, w_up, w_down, layer):
    xn, info = _router(h, ffn_g, w_rg, b_rg, w_re, b_re)
    dest, buf_tok, blk_e, n_used = _dispatch_plan(info, h.shape[0])
    xs = _gather_rows(xn, buf_tok, n_used)
    ys = _expert_mlp(xs, blk_e, n_used, w_gate, w_up, w_down, layer)
    return _combine_rows(h, ys, dest, info)


def _tile_gain(g, n_heads, scale=1.0):
    return jnp.tile(g.astype(F32) * scale, n_heads)


def _mem_kv(mem2d, g_mem, w_mkv, gk):
    width = N_MEM_HEADS * HEAD_DIM
    gains = jnp.concatenate([_tile_gain(gk, N_MEM_HEADS), jnp.ones((width,), F32)])
    modes = jnp.array([1, 0], jnp.int32)
    return _norm_proj(mem2d, g_mem, w_mkv.astype(BF16), gains, modes, tn=width)


def kernel(x, mem, rel_bias, attn_norm_g, ffn_norm_g, mem_norm_g, w_mem_kv, q_norm_mem, k_norm_mem, w_out, w_in_a, q_norm_a, k_norm_a, kv_norm_g, w_kv_b, k_norm_b, w_in_b, q_norm_b, lambda_b, subln_g, w_router_group, b_router_group, w_router_expert, b_router_expert, w_gate, w_up, w_down):
    bsz, seq, dm = x.shape
    n_tok = bsz * seq
    n_mem = mem.shape[1]
    d = HEAD_DIM
    mem_width = N_MEM_HEADS * d
    mix_width = dm - mem_width
    moba_heads = mix_width // d
    diff_heads = mix_width // (2 * d)
    qk_w = diff_heads * d
    tn = 512

    h = x.reshape(n_tok, dm)
    mem2d = mem.reshape(bsz * n_mem, dm)
    bias_tiles = _near_bias_tiles(rel_bias)

    gains = jnp.concatenate([
        _tile_gain(q_norm_a[0], moba_heads, ATTN_SCALE),
        _tile_gain(k_norm_a[0], moba_heads),
        jnp.ones((mix_width,), F32),
        _tile_gain(q_norm_mem[0], N_MEM_HEADS, ATTN_SCALE)])
    modes = jnp.array([1] * (2 * mix_width // tn) + [0] * (mix_width // tn) + [1] * (mem_width // tn),
                      jnp.int32)
    proj = _norm_proj(h, attn_norm_g[0], w_in_a[0].astype(BF16), gains, modes, tn=tn)
    proj = proj.reshape(bsz, seq, -1)
    kvm = _mem_kv(mem2d, mem_norm_g[0], w_mem_kv[0], k_norm_mem[0]).reshape(bsz, n_mem, -1)
    o_mix = _moba_attention(proj, bias_tiles, n_heads=moba_heads, q_off=0, k_off=moba_heads,
                            v_off=2 * moba_heads)
    o_mem = _mem_attention(proj, kvm, q_off=3 * moba_heads)
    h = _out_proj(o_mix.reshape(n_tok, -1), o_mem.reshape(n_tok, -1), w_out[0].astype(BF16), h)
    h = _hier_moe_residual(h, ffn_norm_g[0], w_router_group[0], b_router_group[0],
                           w_router_expert[0], b_router_expert[0], w_gate, w_up, w_down, 0)

    gains = jnp.concatenate([
        _tile_gain(k_norm_b[0], diff_heads), _tile_gain(k_norm_b[1], diff_heads),
        jnp.ones((diff_heads * DIFF_VDIM,), F32)])
    modes = jnp.array([1] * (2 * qk_w // tn) + [0] * (diff_heads * DIFF_VDIM // tn), jnp.int32)
    kv = _norm_proj(h, kv_norm_g, w_kv_b.astype(BF16), gains, modes, tn=tn).reshape(bsz, seq, -1)

    layer = 1
    lam_init = 0.8 - 0.6 * math.exp(-0.3 * layer)
    lam_vec = lambda_b[0].astype(F32)
    lam = (jnp.exp(jnp.sum(lam_vec[0] * lam_vec[1])) - jnp.exp(jnp.sum(lam_vec[2] * lam_vec[3]))
           + lam_init)
    gains = jnp.concatenate([
        _tile_gain(q_norm_b[0, 0], diff_heads, ATTN_SCALE),
        _tile_gain(q_norm_b[0, 1], diff_heads, ATTN_SCALE),
        _tile_gain(q_norm_mem[1], N_MEM_HEADS, ATTN_SCALE)])
    modes = jnp.ones((dm // tn,), jnp.int32)
    proj = _norm_proj(h, attn_norm_g[1], w_in_b[0].astype(BF16), gains, modes, tn=tn)
    proj = proj.reshape(bsz, seq, -1)
    kvm = _mem_kv(mem2d, mem_norm_g[1], w_mem_kv[1], k_norm_mem[1]).reshape(bsz, n_mem, -1)
    diff_bias = bias_tiles.reshape(diff_heads, 2, 2, MOBA_BLOCK, MOBA_BLOCK)
    o_mix = _diff_attention(proj, kv, diff_bias, lam, subln_g[0], n_heads=diff_heads,
                            out_scale=1.0 - lam_init)
    o_mem = _mem_attention(proj, kvm, q_off=2 * diff_heads)
    h = _out_proj(o_mix.reshape(n_tok, -1), o_mem.reshape(n_tok, -1), w_out[1].astype(BF16), h)
    h = _hier_moe_residual(h, ffn_norm_g[1], w_router_group[1], b_router_group[1],
                           w_router_expert[1], b_router_expert[1], w_gate, w_up, w_down, 1)
    return h.reshape(bsz, seq, dm)
```

```python
import functools
import math

import jax
import jax.numpy as jnp
from jax import lax
from jax.experimental import pallas as pl
from jax.experimental.pallas import tpu as pltpu

F32 = jnp.float32
BF16 = jnp.bfloat16

HEAD_DIM = 128
LANES = 128
N_MEM_HEADS = 4
MOBA_BLOCK = 256
MOBA_TOPK = 3
ATTN_TQ = 4 * MOBA_BLOCK
DIFF_VDIM = 2 * HEAD_DIM
REL_BUCKETS = 32
REL_MAX_DIST = 128
N_GROUPS = 8
EXPERTS_PER_GROUP = 8
N_EXPERTS = N_GROUPS * EXPERTS_PER_GROUP
MOE_BLOCK = 256
RMS_EPS = 1e-6
ATTN_SCALE = HEAD_DIM ** -0.5
MASKED = -1e30
VMEM_LIMIT = 56 * 1024 * 1024


def _cparams(*sem):
    return pltpu.CompilerParams(dimension_semantics=sem, vmem_limit_bytes=VMEM_LIMIT)


def _dot_t(a, b):
    return lax.dot_general(a, b, (((1,), (1,)), ((), ())), preferred_element_type=F32)


def _dot(a, b):
    return jnp.dot(a, b, preferred_element_type=F32)


def _norm_proj_kernel(mode_ref, h_ref, g_ref, w_ref, gain_ref, o_ref, xn_ref):
    j = pl.program_id(1)

    @pl.when(j == 0)
    def _():
        x = h_ref[...]
        ms = jnp.mean(x * x, axis=-1, keepdims=True)
        xn_ref[...] = (x * lax.rsqrt(ms + RMS_EPS) * g_ref[...]).astype(BF16)

    acc = _dot(xn_ref[...], w_ref[...])
    n_heads = acc.shape[1] // HEAD_DIM

    @pl.when(mode_ref[j] == 1)
    def _():
        for hh in range(n_heads):
            sl = slice(hh * HEAD_DIM, (hh + 1) * HEAD_DIM)
            a = acc[:, sl]
            ms = jnp.mean(a * a, axis=-1, keepdims=True)
            o_ref[:, sl] = (a * lax.rsqrt(ms + RMS_EPS) * gain_ref[:, sl]).astype(o_ref.dtype)

    @pl.when(mode_ref[j] != 1)
    def _():
        o_ref[...] = acc.astype(o_ref.dtype)


def _norm_proj(h, g, w, gains, modes, *, tn=512):
    n, d = h.shape
    n_out = w.shape[1]
    tm = min(1024, n)
    grid = (n // tm, n_out // tn)
    return pl.pallas_call(
        _norm_proj_kernel,
        grid_spec=pltpu.PrefetchScalarGridSpec(
            num_scalar_prefetch=1,
            grid=grid,
            in_specs=[
                pl.BlockSpec((tm, d), lambda i, j, m: (i, 0)),
                pl.BlockSpec((1, d), lambda i, j, m: (0, 0)),
                pl.BlockSpec((d, tn), lambda i, j, m: (0, j)),
                pl.BlockSpec((1, tn), lambda i, j, m: (0, j)),
            ],
            out_specs=pl.BlockSpec((tm, tn), lambda i, j, m: (i, j)),
            scratch_shapes=[pltpu.VMEM((tm, d), BF16)],
        ),
        out_shape=jax.ShapeDtypeStruct((n, n_out), BF16),
        compiler_params=_cparams("parallel", "arbitrary"),
        name="norm_proj",
    )(modes, h, g.reshape(1, d).astype(F32), w, gains.reshape(1, n_out).astype(F32))


def _t5_bucket(dist):
    n = jnp.maximum(dist, 0)
    max_exact = REL_BUCKETS // 2
    large = max_exact + (jnp.log(jnp.maximum(n, 1).astype(F32) / max_exact)
                         / math.log(REL_MAX_DIST / max_exact)
                         * (REL_BUCKETS - max_exact)).astype(jnp.int32)
    large = jnp.minimum(large, REL_BUCKETS - 1)
    return jnp.where(n < max_exact, n, large)


def _near_bias_tiles(rel_bias):
    t = MOBA_BLOCK
    n_maps = rel_bias.shape[1]
    tab = rel_bias.T.astype(F32)
    by_dist = tab[:, _t5_bucket(jnp.arange(2 * t))] - tab[:, REL_BUCKETS - 1:]
    masked = jnp.full((n_maps, t), MASKED, F32)
    own_by_diff = jnp.concatenate([by_dist[:, :1], masked, by_dist[:, t - 1:0:-1]], axis=1)
    prev_by_diff = jnp.concatenate([by_dist[:, t:0:-1], by_dist[:, :1], by_dist[:, :t:-1]], axis=1)

    def toeplitz(v):
        return jnp.tile(v, (1, t))[:, :t * (2 * t - 1)].reshape(n_maps, t, 2 * t - 1)[:, :, :t]

    return jnp.stack([toeplitz(own_by_diff), toeplitz(prev_by_diff)], axis=1)


def _flash_update(m_ref, l_ref, acc_ref, idx, s, v):
    m_old = m_ref[idx]
    m_new = jnp.maximum(m_old, jnp.max(s, axis=-1, keepdims=True))
    alpha = jnp.exp(m_old - m_new)
    p = jnp.exp(s - m_new)
    if l_ref is not None:
        l_ref[idx] = alpha * l_ref[idx] + jnp.sum(p, axis=-1, keepdims=True)
    acc_ref[idx] = alpha * acc_ref[idx] + _dot(p.astype(BF16), v)
    m_ref[idx] = m_new


def _flash_init(m_ref, l_ref, acc_ref):
    m_ref[...] = jnp.full(m_ref.shape, -3e38, F32)
    if l_ref is not None:
        l_ref[...] = jnp.zeros_like(l_ref)
    acc_ref[...] = jnp.zeros_like(acc_ref)


def _add_corner(s, corner):
    w = s.shape[1] - MOBA_BLOCK
    return jnp.concatenate([s[:, :w], s[:, w:] + corner], axis=1)


def _add_near_bias(s, a, prev_tile, own_tile):
    t = MOBA_BLOCK
    pieces = []
    if a >= 2:
        pieces.append(s[:, :(a - 1) * t])
    if a >= 1:
        pieces.append(s[:, (a - 1) * t:a * t] + prev_tile)
    pieces.append(s[:, a * t:] + own_tile)
    return pieces[0] if len(pieces) == 1 else jnp.concatenate(pieces, axis=1)


def _moba_kernel(q_ref, k_ref, v_ref, bias_ref, o_ref,
                 kmean_ref, qaug_ref, kaug_ref, m_ref, acc_ref, *, nb):
    ti = pl.program_id(2)
    t = MOBA_BLOCK
    tq = ATTN_TQ
    n_sub = tq // t
    d = q_ref.shape[2]
    every = slice(None)

    def v_aug(start, n):
        return jnp.concatenate([v_ref[0, pl.ds(start, n), :], jnp.ones((n, LANES), BF16)], axis=1)

    @pl.when(ti == 0)
    def _():
        kmean_ref[...] = jnp.zeros_like(kmean_ref)
        for jb in range(nb):
            kb = k_ref[0, jb * t:(jb + 1) * t, :].astype(F32)
            kmean_ref[jb:jb + 1, :] = jnp.mean(kb, axis=0, keepdims=True)

    km = kmean_ref[...]
    km_hi = km.astype(BF16)
    km_lo = (km - km_hi.astype(F32)).astype(BF16)
    lane_i = lax.broadcasted_iota(jnp.int32, (t, LANES), 1)
    lane = lane_i.astype(F32)
    neg = jnp.float32(-3e38)
    for a in range(n_sub):
        rows = slice(a * t, (a + 1) * t)
        q = q_ref[0, rows, :]
        own = ti * n_sub + a
        gate = _dot_t(q, km_hi) + _dot_t(q, km_lo)
        gcur = jnp.where(lane_i < own, gate, neg)
        sel = lane_i == own
        for _ in range(MOBA_TOPK):
            mx = jnp.max(gcur, axis=-1, keepdims=True)
            idx = jnp.min(jnp.where(gcur == mx, lane, float(LANES)), axis=-1, keepdims=True)
            hit = lane == idx
            sel = sel | (hit & (mx > 0.5 * neg))
            gcur = jnp.where(hit, neg, gcur)
        pen = jnp.where(sel, 0.0, MASKED).astype(BF16)
        qaug_ref[rows, :] = jnp.concatenate([q, pen], axis=1)

    _flash_init(m_ref, None, acc_ref)

    def build_kaug(c):
        start = pl.multiple_of(c * tq, tq)
        kb = k_ref[0, pl.ds(start, tq), :]
        blk = c * n_sub + lax.broadcasted_iota(jnp.int32, (tq, LANES), 0) // t
        onehot = (lax.broadcasted_iota(jnp.int32, (tq, LANES), 1) == blk).astype(BF16)
        kaug_ref[...] = jnp.concatenate([kb, onehot], axis=1)
        return start

    def far_chunk(c, carry):
        start = build_kaug(c)
        v = v_aug(start, tq)
        for a in range(n_sub):
            rows = slice(a * t, (a + 1) * t)
            s = _dot_t(qaug_ref[rows, :], kaug_ref[...])
            if a == 0:
                s = _add_corner(s, jnp.where(c == ti - 1, bias_ref[0, 1], 0.0))
            _flash_update(m_ref, None, acc_ref, (rows, every), s, v)
        return carry

    lax.fori_loop(0, ti, far_chunk, 0)

    start = build_kaug(ti)
    for a in range(n_sub):
        rows = slice(a * t, (a + 1) * t)
        nk = (a + 1) * t
        s = _dot_t(qaug_ref[rows, :], kaug_ref[:nk, :])
        s = _add_near_bias(s, a, bias_ref[0, 1], bias_ref[0, 0])
        _flash_update(m_ref, None, acc_ref, (rows, every), s, v_aug(start, nk))

    acc = acc_ref[...]
    o_ref[0] = (acc[:, :d] / acc[:, d:]).astype(o_ref.dtype)


def _moba_attention(proj, bias_tiles, *, n_heads, q_off, k_off, v_off):
    b, s, _ = proj.shape
    t = MOBA_BLOCK
    tq = ATTN_TQ
    nb = s // t
    d = HEAD_DIM
    return pl.pallas_call(
        functools.partial(_moba_kernel, nb=nb),
        grid=(b, n_heads, s // tq),
        in_specs=[
            pl.BlockSpec((1, tq, d), lambda bb, h, i: (bb, i, q_off + h)),
            pl.BlockSpec((1, s, d), lambda bb, h, i: (bb, 0, k_off + h)),
            pl.BlockSpec((1, s, d), lambda bb, h, i: (bb, 0, v_off + h)),
            pl.BlockSpec((1, 2, t, t), lambda bb, h, i: (h, 0, 0, 0)),
        ],
        out_specs=pl.BlockSpec((1, tq, d), lambda bb, h, i: (bb, i, h)),
        out_shape=jax.ShapeDtypeStruct((b, s, n_heads * d), BF16),
        scratch_shapes=[
            pltpu.VMEM((LANES, d), F32),
            pltpu.VMEM((tq, d + LANES), BF16),
            pltpu.VMEM((tq, d + LANES), BF16),
            pltpu.VMEM((tq, 1), F32),
            pltpu.VMEM((tq, d + LANES), F32),
        ],
        compiler_params=_cparams("parallel", "parallel", "arbitrary"),
        name="moba_attention",
    )(proj, proj, proj, bias_tiles)


def _diff_kernel(lam_ref, q1_ref, q2_ref, k1_ref, k2_ref, v_ref, bias_ref, g_ref, o_ref,
                 m_ref, l_ref, acc_ref, *, out_scale):
    ti = pl.program_id(2)
    t = MOBA_BLOCK
    tq = ATTN_TQ
    n_sub = tq // t
    every = slice(None)
    q_refs = (q1_ref, q2_ref)
    k_refs = (k1_ref, k2_ref)

    _flash_init(m_ref, l_ref, acc_ref)

    def far_chunk(c, carry):
        start = pl.multiple_of(c * tq, tq)
        v = v_ref[0, pl.ds(start, tq), :]
        for mi in range(2):
            k = k_refs[mi][0, pl.ds(start, tq), :]
            for a in range(n_sub):
                rows = slice(a * t, (a + 1) * t)
                s = _dot_t(q_refs[mi][0, rows, :], k)
                if a == 0:
                    s = _add_corner(s, jnp.where(c == ti - 1, bias_ref[0, mi, 1], 0.0))
                _flash_update(m_ref, l_ref, acc_ref, (mi, rows, every), s, v)
        return carry

    lax.fori_loop(0, ti, far_chunk, 0)

    start = pl.multiple_of(ti * tq, tq)
    for mi in range(2):
        for a in range(n_sub):
            rows = slice(a * t, (a + 1) * t)
            nk = (a + 1) * t
            s = _dot_t(q_refs[mi][0, rows, :], k_refs[mi][0, pl.ds(start, nk), :])
            s = _add_near_bias(s, a, bias_ref[0, mi, 1], bias_ref[0, mi, 0])
            _flash_update(m_ref, l_ref, acc_ref, (mi, rows, every), s,
                          v_ref[0, pl.ds(start, nk), :])

    o = acc_ref[0] / l_ref[0] - lam_ref[0] * (acc_ref[1] / l_ref[1])
    ms = jnp.mean(o * o, axis=-1, keepdims=True)
    o_ref[0] = (o * lax.rsqrt(ms + RMS_EPS) * g_ref[...] * out_scale).astype(o_ref.dtype)


def _diff_attention(proj, kv, bias_tiles, lam, subln_g, *, n_heads, out_scale):
    b, s, _ = proj.shape
    t = MOBA_BLOCK
    tq = ATTN_TQ
    d = HEAD_DIM
    dv = DIFF_VDIM
    return pl.pallas_call(
        functools.partial(_diff_kernel, out_scale=out_scale),
        grid=(b, n_heads, s // tq),
        in_specs=[
            pl.BlockSpec(memory_space=pltpu.SMEM),
            pl.BlockSpec((1, tq, d), lambda bb, h, i: (bb, i, h)),
            pl.BlockSpec((1, tq, d), lambda bb, h, i: (bb, i, n_heads + h)),
            pl.BlockSpec((1, s, d), lambda bb, h, i: (bb, 0, h)),
            pl.BlockSpec((1, s, d), lambda bb, h, i: (bb, 0, n_heads + h)),
            pl.BlockSpec((1, s, dv), lambda bb, h, i: (bb, 0, n_heads + h)),
            pl.BlockSpec((1, 2, 2, t, t), lambda bb, h, i: (h, 0, 0, 0, 0)),
            pl.BlockSpec((1, dv), lambda bb, h, i: (0, 0)),
        ],
        out_specs=pl.BlockSpec((1, tq, dv), lambda bb, h, i: (bb, i, h)),
        out_shape=jax.ShapeDtypeStruct((b, s, n_heads * dv), BF16),
        scratch_shapes=[
            pltpu.VMEM((2, tq, 1), F32),
            pltpu.VMEM((2, tq, 1), F32),
            pltpu.VMEM((2, tq, dv), F32),
        ],
        compiler_params=_cparams("parallel", "parallel", "arbitrary"),
        name="diff_attention",
    )(lam.reshape(1).astype(F32), proj, proj, kv, kv, kv, bias_tiles,
      subln_g.reshape(1, dv).astype(F32))


def _mem_attn_kernel(q_ref, k_ref, v_ref, o_ref):
    s = _dot_t(q_ref[0], k_ref[0])
    m = jnp.max(s, axis=-1, keepdims=True)
    p = jnp.exp(s - m)
    l = jnp.sum(p, axis=-1, keepdims=True)
    o_ref[0] = (_dot(p.astype(BF16), v_ref[0]) / l).astype(o_ref.dtype)


def _mem_attention(proj, kvm, *, q_off):
    b, s, _ = proj.shape
    n_mem = kvm.shape[1]
    d = HEAD_DIM
    tq = min(1024, s)
    return pl.pallas_call(
        _mem_attn_kernel,
        grid=(b, N_MEM_HEADS, s // tq),
        in_specs=[
            pl.BlockSpec((1, tq, d), lambda bb, h, i: (bb, i, q_off + h)),
            pl.BlockSpec((1, n_mem, d), lambda bb, h, i: (bb, 0, h)),
            pl.BlockSpec((1, n_mem, d), lambda bb, h, i: (bb, 0, N_MEM_HEADS + h)),
        ],
        out_specs=pl.BlockSpec((1, tq, d), lambda bb, h, i: (bb, i, h)),
        out_shape=jax.ShapeDtypeStruct((b, s, N_MEM_HEADS * d), BF16),
        compiler_params=_cparams("parallel", "parallel", "arbitrary"),
        name="mem_attention",
    )(proj, kvm, kvm)


def _out_proj_kernel(a_ref, b_ref, wa_ref, wb_ref, h_ref, o_ref):
    o_ref[...] = h_ref[...] + _dot(a_ref[...], wa_ref[...]) + _dot(b_ref[...], wb_ref[...])


def _out_proj(o_mix, o_mem, w_out, h, *, tm=512, tn=2048):
    n, d = h.shape
    wa = o_mix.shape[1]
    wb = o_mem.shape[1]
    return pl.pallas_call(
        _out_proj_kernel,
        grid=(n // tm, d // tn),
        in_specs=[
            pl.BlockSpec((tm, wa), lambda i, j: (i, 0)),
            pl.BlockSpec((tm, wb), lambda i, j: (i, 0)),
            pl.BlockSpec((wa, tn), lambda i, j: (0, j)),
            pl.BlockSpec((wb, tn), lambda i, j: (0, j)),
            pl.BlockSpec((tm, tn), lambda i, j: (i, j)),
        ],
        out_specs=pl.BlockSpec((tm, tn), lambda i, j: (i, j)),
        out_shape=jax.ShapeDtypeStruct((n, d), F32),
        compiler_params=_cparams("parallel", "arbitrary"),
        name="out_proj",
    )(o_mix, o_mem, w_out[:wa], w_out[wa:], h)


def _pack_halves(x):
    w = x.shape[1] // 2
    bits = lax.bitcast_convert_type(x.astype(BF16).astype(F32), jnp.uint32)
    return (bits[:, :w] >> 16) | (bits[:, w:] & jnp.uint32(0xFFFF0000))


def _unpack_halves(words):
    lo = lax.bitcast_convert_type(words << 16, F32)
    hi = lax.bitcast_convert_type(words & jnp.uint32(0xFFFF0000), F32)
    return lo, hi


def _router_kernel(h_ref, g_ref, wh_ref, wl_ref, b_ref, xn_ref, info_ref):
    x = h_ref[...]
    ms = jnp.mean(x * x, axis=-1, keepdims=True)
    xn = x * lax.rsqrt(ms + RMS_EPS) * g_ref[...]
    xn_ref[...] = _pack_halves(xn)
    xh = xn.astype(BF16)
    xl = (xn - xh.astype(F32)).astype(BF16)
    logits = _dot(xh, wh_ref[...]) + _dot(xl, wh_ref[...]) + _dot(xh, wl_ref[...]) + b_ref[...]

    tm = logits.shape[0]
    lane_i = lax.broadcasted_iota(jnp.int32, (tm, LANES), 1)
    lane = lane_i.astype(F32)
    neg = jnp.float32(-3e38)
    big = float(LANES)

    def top1(vals):
        mx = jnp.max(vals, axis=-1, keepdims=True)
        idx = jnp.min(jnp.where(vals == mx, lane, big), axis=-1, keepdims=True)
        return mx, idx

    is_group = lane_i < N_GROUPS
    gl = jnp.where(is_group, logits, neg)
    gmax, gidx = top1(gl)
    p_g = 1.0 / jnp.sum(jnp.where(is_group, jnp.exp(logits - gmax), 0.0), axis=-1, keepdims=True)

    e_lo = N_GROUPS + gidx * EXPERTS_PER_GROUP
    member = (lane >= e_lo) & (lane < e_lo + EXPERTS_PER_GROUP)
    el = jnp.where(member, logits, neg)
    v1, i1 = top1(el)
    v2, i2 = top1(jnp.where(lane == i1, neg, el))
    w2 = jnp.exp(v2 - v1)
    w1 = 1.0 / (1.0 + w2)
    info = jnp.where(lane_i == 0, i1 - N_GROUPS, 0.0)
    info = jnp.where(lane_i == 1, i2 - N_GROUPS, info)
    info = jnp.where(lane_i == 2, p_g * w1, info)
    info = jnp.where(lane_i == 3, p_g * (w2 * w1), info)
    info_ref[...] = info


def _router(h, g, w_rg, b_rg, w_re, b_re, *, tm=512):
    n, d = h.shape
    pad = LANES - N_GROUPS - N_EXPERTS
    w = jnp.concatenate([w_rg, w_re, jnp.zeros((d, pad), F32)], axis=1).astype(F32)
    bias = jnp.concatenate([b_rg, b_re, jnp.zeros((pad,), F32)]).astype(F32).reshape(1, LANES)
    w_hi = w.astype(BF16)
    w_lo = (w - w_hi.astype(F32)).astype(BF16)
    return pl.pallas_call(
        _router_kernel,
        grid=(n // tm,),
        in_specs=[
            pl.BlockSpec((tm, d), lambda i: (i, 0)),
            pl.BlockSpec((1, d), lambda i: (0, 0)),
            pl.BlockSpec((d, LANES), lambda i: (0, 0)),
            pl.BlockSpec((d, LANES), lambda i: (0, 0)),
            pl.BlockSpec((1, LANES), lambda i: (0, 0)),
        ],
        out_specs=[
            pl.BlockSpec((tm, d // 2), lambda i: (i, 0)),
            pl.BlockSpec((tm, LANES), lambda i: (i, 0)),
        ],
        out_shape=[jax.ShapeDtypeStruct((n, d // 2), jnp.uint32),
                   jax.ShapeDtypeStruct((n, LANES), F32)],
        compiler_params=_cparams("parallel"),
        name="moe_router",
    )(h, g.reshape(1, d).astype(F32), w_hi, w_lo, bias)


def _rank_kernel(info_ref, pstart_ref, dest_ref, run_ref):
    i = pl.program_id(0)

    @pl.when(i == 0)
    def _():
        run_ref[...] = jnp.zeros_like(run_ref)

    info = info_ref[...]
    tm = info.shape[0]
    lane = lax.broadcasted_iota(jnp.int32, (tm, LANES), 1).astype(F32)
    oh1 = lane == info[:, 0:1]
    oh2 = lane == info[:, 1:2]
    both = (oh1 | oh2).astype(BF16)
    earlier = (lax.broadcasted_iota(jnp.int32, (tm, tm), 1)
               < lax.broadcasted_iota(jnp.int32, (tm, tm), 0)).astype(BF16)
    base = _dot(earlier, both) + run_ref[...] + pstart_ref[...]
    d1 = jnp.sum(jnp.where(oh1, base, 0.0), axis=-1, keepdims=True)
    d2 = jnp.sum(jnp.where(oh2, base, 0.0), axis=-1, keepdims=True)
    run_ref[...] += jnp.sum(both.astype(F32), axis=0, keepdims=True)
    lane_i = lax.broadcasted_iota(jnp.int32, (tm, LANES), 1)
    dest = jnp.where(lane_i == 0, d1, jnp.where(lane_i == 1, d2, 0.0))
    dest_ref[...] = dest.astype(jnp.int32)


def _dispatch_plan(info, n_tok, *, tm=512):
    n_asg = 2 * n_tok
    n_blocks = -(-n_asg // MOE_BLOCK) + N_EXPERTS
    e12 = info[:, :2].astype(jnp.int32)
    onehot = e12[:, :, None] == jnp.arange(N_EXPERTS, dtype=jnp.int32)[None, None, :]
    counts = jnp.sum(onehot, axis=(0, 1), dtype=jnp.int32)
    padded = (counts + MOE_BLOCK - 1) // MOE_BLOCK * MOE_BLOCK
    pends = jnp.cumsum(padded)
    pstarts = pends - padded
    blk_e = jnp.minimum(
        jnp.sum(pends[None, :] <= (jnp.arange(n_blocks, dtype=jnp.int32) * MOE_BLOCK)[:, None],
                axis=1), N_EXPERTS - 1).astype(jnp.int32)
    n_used = (pends[-1] // MOE_BLOCK).astype(jnp.int32).reshape(1)
    pstart_row = jnp.zeros((1, LANES), F32).at[0, :N_EXPERTS].set(pstarts.astype(F32))
    dest = pl.pallas_call(
        _rank_kernel,
        grid=(n_tok // tm,),
        in_specs=[
            pl.BlockSpec((tm, LANES), lambda i: (i, 0)),
            pl.BlockSpec((1, LANES), lambda i: (0, 0)),
        ],
        out_specs=pl.BlockSpec((tm, LANES), lambda i: (i, 0)),
        out_shape=jax.ShapeDtypeStruct((n_tok, LANES), jnp.int32),
        scratch_shapes=[pltpu.VMEM((1, LANES), F32)],
        compiler_params=_cparams("arbitrary"),
        name="moe_rank",
    )(info, pstart_row)
    dest = dest[:, :2]
    buf_tok = jnp.zeros((n_blocks * MOE_BLOCK,), jnp.int32).at[dest.reshape(-1)].set(
        jnp.arange(n_asg, dtype=jnp.int32) // 2)
    return (dest.reshape(n_tok // MOE_BLOCK, 1, 2 * MOE_BLOCK),
            buf_tok.reshape(n_blocks, 1, MOE_BLOCK), blk_e, n_used)


def _gather_rows_kernel(n_used_ref, tok_ref, x_hbm, o_ref, xb0, xb1, sem):
    b = pl.program_id(0)
    n_used = n_used_ref[0]
    rows = MOE_BLOCK
    xbufs = (xb0, xb1)

    def row_copy(p, r, t):
        return pltpu.make_async_copy(x_hbm.at[pl.ds(t, 1), :], xbufs[p].at[pl.ds(r, 1), :],
                                     sem.at[p])

    def start_gather(blk, p):
        base = blk * rows
        for r in range(rows):
            row_copy(p, r, tok_ref[base + r]).start()

    def wait_gather(p):
        for r in range(rows):
            row_copy(p, r, 0).wait()

    @pl.when(b == 0)
    def _():
        start_gather(0, 0)

        @pl.when(n_used > 1)
        def _():
            start_gather(1, 1)

    for p in (0, 1):
        @pl.when((b % 2 == p) & (b < n_used))
        def _():
            wait_gather(p)
            o_ref[...] = xbufs[p][...]

            @pl.when(b + 2 < n_used)
            def _():
                start_gather(b + 2, p)

    @pl.when(b >= n_used)
    def _():
        o_ref[...] = jnp.zeros_like(o_ref)


def _gather_rows(xn_packed, buf_tok, n_used):
    half = xn_packed.shape[1]
    n_blocks = buf_tok.shape[0]

    return pl.pallas_call(
        _gather_rows_kernel,
        grid_spec=pltpu.PrefetchScalarGridSpec(
            num_scalar_prefetch=2,
            grid=(n_blocks,),
            in_specs=[pl.BlockSpec(memory_space=pl.ANY)],
            out_specs=pl.BlockSpec((MOE_BLOCK, half), lambda b, nu, tok: (b, 0)),
            scratch_shapes=[
                pltpu.VMEM((MOE_BLOCK, half), jnp.uint32),
                pltpu.VMEM((MOE_BLOCK, half), jnp.uint32),
                pltpu.SemaphoreType.DMA((2,)),
            ],
        ),
        out_shape=jax.ShapeDtypeStruct((n_blocks * MOE_BLOCK, half), jnp.uint32),
        compiler_params=_cparams("arbitrary"),
        name="moe_gather_rows",
    )(n_used, buf_tok.reshape(-1), xn_packed)


def _expert_mlp_kernel(blk_e_ref, n_used_ref, xs_ref, wg_ref, wu_ref, wd_ref, ys_ref,
                       wg_bf, wu_bf, wd_bf):
    b = pl.program_id(0)

    @pl.when(b < n_used_ref[0])
    def _():
        e = blk_e_ref[b]
        e_prev = blk_e_ref[jnp.maximum(b - 1, 0)]

        @pl.when((b == 0) | (e != e_prev))
        def _():
            wg_bf[...] = wg_ref[0, 0].astype(BF16)
            wu_bf[...] = wu_ref[0, 0].astype(BF16)
            wd_bf[...] = wd_ref[0, 0].astype(BF16)

        x_lo, x_hi = _unpack_halves(xs_ref[...])
        x_lo = x_lo.astype(BF16)
        x_hi = x_hi.astype(BF16)
        half = x_lo.shape[1]
        hg = _dot(x_lo, wg_bf[:half, :]) + _dot(x_hi, wg_bf[half:, :])
        hu = _dot(x_lo, wu_bf[:half, :]) + _dot(x_hi, wu_bf[half:, :])
        hdn = (hg * jax.nn.sigmoid(hg)) * hu
        ys_ref[...] = _pack_halves(_dot(hdn.astype(BF16), wd_bf[...]))

    @pl.when(b >= n_used_ref[0])
    def _():
        ys_ref[...] = jnp.zeros_like(ys_ref)


def _expert_mlp(xs, blk_e, n_used, w_gate, w_up, w_down, layer):
    n_rows, half = xs.shape
    d = 2 * half
    ff = w_gate.shape[3]
    n_blocks = n_rows // MOE_BLOCK

    def used(b, nu):
        return jnp.minimum(b, nu[0] - 1)

    def w_spec(shape):
        return pl.BlockSpec((1, 1) + shape, lambda b, be, nu: (layer, be[used(b, nu)], 0, 0))

    return pl.pallas_call(
        _expert_mlp_kernel,
        grid_spec=pltpu.PrefetchScalarGridSpec(
            num_scalar_prefetch=2,
            grid=(n_blocks,),
            in_specs=[
                pl.BlockSpec((MOE_BLOCK, half), lambda b, be, nu: (used(b, nu), 0)),
                w_spec((d, ff)),
                w_spec((d, ff)),
                w_spec((ff, d)),
            ],
            out_specs=pl.BlockSpec((MOE_BLOCK, half), lambda b, be, nu: (b, 0)),
            scratch_shapes=[
                pltpu.VMEM((d, ff), BF16),
                pltpu.VMEM((d, ff), BF16),
                pltpu.VMEM((ff, d), BF16),
            ],
        ),
        out_shape=jax.ShapeDtypeStruct((n_rows, half), jnp.uint32),
        compiler_params=_cparams("arbitrary"),
        name="moe_expert_mlp",
    )(blk_e, n_used, xs, w_gate, w_up, w_down)


def _combine_rows_kernel(dest_cur, dest_next, h_ref, info_ref, ys_hbm, o_ref,
                         ya0, yb0, ya1, yb1, sem):
    i = pl.program_id(0)
    nt = pl.num_programs(0)
    rows = MOE_BLOCK
    bufs = ((ya0, yb0), (ya1, yb1))

    def row_copy(p, s, r, d):
        return pltpu.make_async_copy(ys_hbm.at[pl.ds(d, 1), :], bufs[p][s].at[pl.ds(r, 1), :],
                                     sem.at[p])

    def start(dest_ref, p):
        for r in range(rows):
            row_copy(p, 0, r, dest_ref[0, 0, 2 * r]).start()
            row_copy(p, 1, r, dest_ref[0, 0, 2 * r + 1]).start()

    def wait(p):
        for r in range(rows):
            row_copy(p, 0, r, 0).wait()
            row_copy(p, 1, r, 0).wait()

    @pl.when(i == 0)
    def _():
        start(dest_cur, 0)

    for p in (0, 1):
        @pl.when(i % 2 == p)
        def _():
            @pl.when(i + 1 < nt)
            def _():
                start(dest_next, 1 - p)

            wait(p)
            info = info_ref[...]
            g1 = info[:, 2:3]
            g2 = info[:, 3:4]
            a_lo, a_hi = _unpack_halves(bufs[p][0][...])
            b_lo, b_hi = _unpack_halves(bufs[p][1][...])
            half = a_lo.shape[1]
            o_ref[:, :half] = h_ref[:, :half] + (g1 * a_lo + g2 * b_lo)
            o_ref[:, half:] = h_ref[:, half:] + (g1 * a_hi + g2 * b_hi)


def _combine_rows(h, ys, dest, info):
    n, d = h.shape
    tm = MOE_BLOCK
    nt = n // tm

    def dest_spec(index):
        return pl.BlockSpec((1, 1, 2 * tm), lambda i: (index(i), 0, 0), memory_space=pltpu.SMEM)

    return pl.pallas_call(
        _combine_rows_kernel,
        grid=(nt,),
        in_specs=[
            dest_spec(lambda i: i),
            dest_spec(lambda i: jnp.minimum(i + 1, nt - 1)),
            pl.BlockSpec((tm, d), lambda i: (i, 0)),
            pl.BlockSpec((tm, LANES), lambda i: (i, 0)),
            pl.BlockSpec(memory_space=pl.ANY),
        ],
        out_specs=pl.BlockSpec((tm, d), lambda i: (i, 0)),
        out_shape=jax.ShapeDtypeStruct((n, d), F32),
        scratch_shapes=([pltpu.VMEM((tm, d // 2), jnp.uint32)] * 4
                        + [pltpu.SemaphoreType.DMA((2,))]),
        compiler_params=_cparams("arbitrary"),
        name="moe_combine_rows",
    )(dest, dest, h, info, ys)


def _hier_moe_residual(h, ffn_g, w_rg, b_rg, w_re, b_re, w_gate, w_up, w_down, layer):
    xn, info = _router(h, ffn_g, w_rg, b_rg, w_re, b_re)
    dest, buf_tok, blk_e, n_used = _dispatch_plan(info, h.shape[0])
    xs = _gather_rows(xn, buf_tok, n_used)
    ys = _expert_mlp(xs, blk_e, n_used, w_gate, w_up, w_down, layer)
    return _combine_rows(h, ys, dest, info)


def _tile_gain(g, n_heads, scale=1.0):
    return jnp.tile(g.astype(F32) * scale, n_heads)


def _mem_kv(mem2d, g_mem, w_mkv, gk):
    width = N_MEM_HEADS * HEAD_DIM
    gains = jnp.concatenate([_tile_gain(gk, N_MEM_HEADS), jnp.ones((width,), F32)])
    modes = jnp.array([1, 0], jnp.int32)
    return _norm_proj(mem2d, g_mem, w_mkv.astype(BF16), gains, modes, tn=width)


def kernel(x, mem, rel_bias, attn_norm_g, ffn_norm_g, mem_norm_g, w_mem_kv, q_norm_mem, k_norm_mem, w_out, w_in_a, q_norm_a, k_norm_a, kv_norm_g, w_kv_b, k_norm_b, w_in_b, q_norm_b, lambda_b, subln_g, w_router_group, b_router_group, w_router_expert, b_router_expert, w_gate, w_up, w_down):
    bsz, seq, dm = x.shape
    n_tok = bsz * seq
    n_mem = mem.shape[1]
    d = HEAD_DIM
    mem_width = N_MEM_HEADS * d
    mix_width = dm - mem_width
    moba_heads = mix_width // d
    diff_heads = mix_width // (2 * d)
    qk_w = diff_heads * d
    tn = 512

    h = x.reshape(n_tok, dm)
    mem2d = mem.reshape(bsz * n_mem, dm)
    bias_tiles = _near_bias_tiles(rel_bias)

    gains = jnp.concatenate([
        _tile_gain(q_norm_a[0], moba_heads, ATTN_SCALE),
        _tile_gain(k_norm_a[0], moba_heads),
        jnp.ones((mix_width,), F32),
        _tile_gain(q_norm_mem[0], N_MEM_HEADS, ATTN_SCALE)])
    modes = jnp.array([1] * (2 * mix_width // tn) + [0] * (mix_width // tn) + [1] * (mem_width // tn),
                      jnp.int32)
    proj = _norm_proj(h, attn_norm_g[0], w_in_a[0].astype(BF16), gains, modes, tn=tn)
    proj = proj.reshape(bsz, seq, -1)
    kvm = _mem_kv(mem2d, mem_norm_g[0], w_mem_kv[0], k_norm_mem[0]).reshape(bsz, n_mem, -1)
    o_mix = _moba_attention(proj, bias_tiles, n_heads=moba_heads, q_off=0, k_off=moba_heads,
                            v_off=2 * moba_heads)
    o_mem = _mem_attention(proj, kvm, q_off=3 * moba_heads)
    h = _out_proj(o_mix.reshape(n_tok, -1), o_mem.reshape(n_tok, -1), w_out[0].astype(BF16), h)
    h = _hier_moe_residual(h, ffn_norm_g[0], w_router_group[0], b_router_group[0],
                           w_router_expert[0], b_router_expert[0], w_gate, w_up, w_down, 0)

    gains = jnp.concatenate([
        _tile_gain(k_norm_b[0], diff_heads), _tile_gain(k_norm_b[1], diff_heads),
        jnp.ones((diff_heads * DIFF_VDIM,), F32)])
    modes = jnp.array([1] * (2 * qk_w // tn) + [0] * (diff_heads * DIFF_VDIM // tn), jnp.int32)
    kv = _norm_proj(h, kv_norm_g, w_kv_b.astype(BF16), gains, modes, tn=tn).reshape(bsz, seq, -1)

    layer = 1
    lam_init = 0.8 - 0.6 * math.exp(-0.3 * layer)
    lam_vec = lambda_b[0].astype(F32)
    lam = (jnp.exp(jnp.sum(lam_vec[0] * lam_vec[1])) - jnp.exp(jnp.sum(lam_vec[2] * lam_vec[3]))
           + lam_init)
    gains = jnp.concatenate([
        _tile_gain(q_norm_b[0, 0], diff_heads, ATTN_SCALE),
        _tile_gain(q_norm_b[0, 1], diff_heads, ATTN_SCALE),
        _tile_gain(q_norm_mem[1], N_MEM_HEADS, ATTN_SCALE)])
    modes = jnp.ones((dm // tn,), jnp.int32)
    proj = _norm_proj(h, attn_norm_g[1], w_in_b[0].astype(BF16), gains, modes, tn=tn)
    proj = proj.reshape(bsz, seq, -1)
    kvm = _mem_kv(mem2d, mem_norm_g[1], w_mem_kv[1], k_norm_mem[1]).reshape(bsz, n_mem, -1)
    diff_bias = bias_tiles.reshape(diff_heads, 2, 2, MOBA_BLOCK, MOBA_BLOCK)
    o_mix = _diff_attention(proj, kv, diff_bias, lam, subln_g[0], n_heads=diff_heads,
                            out_scale=1.0 - lam_init)
    o_mem = _mem_attention(proj, kvm, q_off=2 * diff_heads)
    h = _out_proj(o_mix.reshape(n_tok, -1), o_mem.reshape(n_tok, -1), w_out[1].astype(BF16), h)
    h = _hier_moe_residual(h, ffn_norm_g[1], w_router_group[1], b_router_group[1],
                           w_router_expert[1], b_router_expert[1], w_gate, w_up, w_down, 1)
    return h.reshape(bsz, seq, dm)
```
